```python
import math
import jax
import jax.numpy as jnp
from jax import lax
import numpy as np

D_MODEL = 1024
BATCH = 32
SEQ = 2048
DEPTH = 2

GRID_W = 64
CTX_LEN = 256
EPS = 1e-6

ATT_HEAD_DIM = 64
ATT_Q_HEADS = (D_MODEL // 2) // ATT_HEAD_DIM
ATT_KV_HEADS = ATT_Q_HEADS // 2
ATT_GROUP = ATT_Q_HEADS // ATT_KV_HEADS
ATT_WIDTH = ATT_Q_HEADS * ATT_HEAD_DIM
ATT_KV_WIDTH = ATT_KV_HEADS * ATT_HEAD_DIM
ATT_IN = ATT_WIDTH + 2 * ATT_KV_WIDTH
Q_BLOCK = 128
ROPE_THETA = 10000.0

S5_WIDTH = D_MODEL // 4
S5_GROUP_CH = 16
S5_GROUPS = S5_WIDTH // S5_GROUP_CH
S5_STATE = 64

RWKV_WIDTH = D_MODEL // 4
RWKV_HEAD = 64
RWKV_HEADS = RWKV_WIDTH // RWKV_HEAD
DECAY_LORA = 64
AAA_LORA = 64
GATE_LORA = 128
RWKV_STREAM = 3 * RWKV_WIDTH + DECAY_LORA + AAA_LORA + GATE_LORA
SHORT_CONV = 3
GN_EPS = 64e-5

N_BRANCH = 3
OFF_S5 = ATT_IN
OFF_RWKV = OFF_S5 + S5_WIDTH
OFF_GATE = OFF_RWKV + RWKV_STREAM
IN_WIDTH = OFF_GATE + N_BRANCH * D_MODEL

N_EXPERT_GROUPS = 4
EXPERTS_PER_GROUP = 8
N_EXPERTS = N_EXPERT_GROUPS * EXPERTS_PER_GROUP
TOP_K = 2
D_EXPERT = D_MODEL // 2
MOE_BLOCK = 256

kernel_name = 'hybrid_attn_s5_rwkv7_hmoe_flow_block'


def rmsnorm(x, g):
    xf = x.astype(jnp.float32)
    y = xf * lax.rsqrt(jnp.mean(xf * xf, axis=-1, keepdims=True) + EPS)
    return (y * g.astype(jnp.float32)).astype(x.dtype)


def grid_positions(n_tokens):
    rows = n_tokens // GRID_W
    row = jnp.repeat(jnp.arange(rows, dtype=jnp.int32), GRID_W)
    col = jnp.tile(jnp.arange(GRID_W, dtype=jnp.int32), rows)
    return row, col


def rope_1d(x, pos):
    half = x.shape[-1] // 2
    inv = ROPE_THETA ** (-jnp.arange(half, dtype=jnp.float32) / half)
    ang = pos.astype(jnp.float32)[:, None] * inv[None, :]
    shape = (1, pos.shape[0]) + (1,) * (x.ndim - 3) + (half,)
    cos = jnp.cos(ang).reshape(shape).astype(x.dtype)
    sin = jnp.sin(ang).reshape(shape).astype(x.dtype)
    x1, x2 = x[..., :half], x[..., half:]
    return jnp.concatenate([x1 * cos - x2 * sin, x2 * cos + x1 * sin], axis=-1)


def axial_rope(x, row, col):
    h = x.shape[-1] // 2
    return jnp.concatenate([rope_1d(x[..., :h], row), rope_1d(x[..., h:], col)], axis=-1)


def attend(q, k, v):
    s = jnp.einsum('bqhgd,bkhd->bhgqk', q, k).astype(jnp.float32) * (ATT_HEAD_DIM ** -0.5)
    p = jax.nn.softmax(s, axis=-1).astype(v.dtype)
    return jnp.einsum('bhgqk,bkhd->bqhgd', p, v)


def attention_branch(a_lat, a_ctx, q_gain, k_gain, row, col, need_ctx):
    B, L, _ = a_lat.shape
    C = a_ctx.shape[1]

    def queries(a):
        q = a[..., :ATT_WIDTH].reshape(a.shape[:2] + (ATT_KV_HEADS, ATT_GROUP, ATT_HEAD_DIM))
        return rmsnorm(q, q_gain)

    def keys_values(a):
        k = a[..., ATT_WIDTH:ATT_WIDTH + ATT_KV_WIDTH].reshape(a.shape[:2] + (ATT_KV_HEADS, ATT_HEAD_DIM))
        v = a[..., ATT_WIDTH + ATT_KV_WIDTH:ATT_IN].reshape(a.shape[:2] + (ATT_KV_HEADS, ATT_HEAD_DIM))
        return rmsnorm(k, k_gain), v

    q_l = axial_rope(queries(a_lat), row, col)
    k_l, v_l = keys_values(a_lat)
    k_l = axial_rope(k_l, row, col)
    k_c, v_c = keys_values(a_ctx)
    k_all = jnp.concatenate([k_l, k_c], axis=1)
    v_all = jnp.concatenate([v_l, v_c], axis=1)
    n_blk = L // Q_BLOCK
    q_blocks = jnp.moveaxis(q_l.reshape((B, n_blk, Q_BLOCK) + q_l.shape[2:]), 1, 0)
    o = lax.map(lambda qb: attend(qb, k_all, v_all), q_blocks)
    o_lat = jnp.moveaxis(o, 0, 1).reshape(B, L, ATT_WIDTH)
    o_ctx = attend(queries(a_ctx), k_c, v_c).reshape(B, C, ATT_WIDTH) if need_ctx else None
    return o_lat, o_ctx


def s5_discretise(a_re, a_im, log_dt, b_re, b_im):
    dt = jnp.exp(log_dt.astype(jnp.float32))[:, None]
    lam_re = a_re.astype(jnp.float32)
    lam_im = a_im.astype(jnp.float32)
    mag = jnp.exp(lam_re * dt)
    ab_re = mag * jnp.cos(lam_im * dt)
    ab_im = mag * jnp.sin(lam_im * dt)
    den = lam_re * lam_re + lam_im * lam_im
    nr = ab_re - 1.0
    z_re = (nr * lam_re + ab_im * lam_im) / den
    z_im = (ab_im * lam_re - nr * lam_im) / den
    br = b_re.astype(jnp.float32)
    bi = b_im.astype(jnp.float32)
    bb_re = z_re[..., None] * br - z_im[..., None] * bi
    bb_im = z_re[..., None] * bi + z_im[..., None] * br
    return ab_re, ab_im, bb_re, bb_im


def complex_affine_combine(left, right):
    a1r, a1i, b1r, b1i = left
    a2r, a2i, b2r, b2i = right
    ar = a2r * a1r - a2i * a1i
    ai = a2r * a1i + a2i * a1r
    br = a2r * b1r - a2i * b1i + b2r
    bi = a2r * b1i + a2i * b1r + b2i
    return ar, ai, br, bi


def s5_scan(u, ab_re, ab_im, bb_re, bb_im, h0, reverse):
    bu_re = jnp.einsum('btgc,gnc->btgn', u, bb_re)
    bu_im = jnp.einsum('btgc,gnc->btgn', u, bb_im)
    a_re = jnp.broadcast_to(ab_re, bu_re.shape)
    a_im = jnp.broadcast_to(ab_im, bu_im.shape)
    acc_re, acc_im, h_re, h_im = lax.associative_scan(
        complex_affine_combine, (a_re, a_im, bu_re, bu_im), reverse=reverse, axis=1)
    if h0 is not None:
        h0_re, h0_im = h0[0][:, None], h0[1][:, None]
        h_re = h_re + acc_re * h0_re - acc_im * h0_im
        h_im = h_im + acc_re * h0_im + acc_im * h0_re
    return h_re, h_im


def s5_readout(h_re, h_im, c_re, c_im):
    return (jnp.einsum('btgn,gcn->btgc', h_re, c_re.astype(jnp.float32))
            - jnp.einsum('btgn,gcn->btgc', h_im, c_im.astype(jnp.float32)))


def s5_branch(s_lat, s_ctx, a_re, a_im, log_dt, b_re, b_im, c_re, c_im, d_skip, w_glu, need_ctx):
    def groups(s):
        return s.astype(jnp.float32).reshape(s.shape[:2] + (S5_GROUPS, S5_GROUP_CH))

    u_l, u_c = groups(s_lat), groups(s_ctx)
    ys_l, ys_c = [], []
    for d, rev in enumerate((False, True)):
        ab_re, ab_im, bb_re, bb_im = s5_discretise(a_re[d], a_im[d], log_dt[d], b_re[d], b_im[d])
        hc_re, hc_im = s5_scan(u_c, ab_re, ab_im, bb_re, bb_im, None, rev)
        edge = 0 if rev else -1
        hl_re, hl_im = s5_scan(u_l, ab_re, ab_im, bb_re, bb_im, (hc_re[:, edge], hc_im[:, edge]), rev)
        ys_l.append(s5_readout(hl_re, hl_im, c_re[d], c_im[d]))
        if need_ctx:
            ys_c.append(s5_readout(hc_re, hc_im, c_re[d], c_im[d]))

    def finish(ys, s):
        y = (ys[0] + ys[1]).reshape(s.shape) + d_skip.astype(jnp.float32) * s.astype(jnp.float32)
        z = jax.nn.gelu(y) @ w_glu.astype(jnp.float32)
        return (z[..., :S5_WIDTH] * jax.nn.sigmoid(z[..., S5_WIDTH:])).astype(s.dtype)

    return finish(ys_l, s_lat), (finish(ys_c, s_ctx) if need_ctx else None)


def short_conv(x, w):
    xp = jnp.pad(x, ((0, 0), (1, 1), (0, 0)))
    return xp[:, :-2] * w[0] + xp[:, 1:-1] * w[1] + xp[:, 2:] * w[2]


def rwkv_scan(r, decay, k, v, kk, kka, s0, reverse):
    def step(S, inp):
        r_t, w_t, k_t, v_t, kk_t, kka_t = inp
        sa = jnp.einsum('bhij,bhj->bhi', S, kk_t)
        S = S * w_t[:, :, None, :] - sa[..., None] * kka_t[:, :, None, :] + v_t[..., None] * k_t[:, :, None, :]
        y = None if r_t is None else jnp.einsum('bhij,bhj->bhi', S, r_t)
        return S, y

    def seq_first(t):
        return None if t is None else jnp.swapaxes(t, 0, 1)

    xs = tuple(seq_first(t) for t in (r, decay, k, v, kk, kka))
    s_fin, ys = lax.scan(step, s0, xs, reverse=reverse)
    return s_fin, seq_first(ys)


def rwkv_branch(z_lat, z_ctx, conv_w, w0, w2, a0, a2, g2, k_k, k_a, r_k, ln_w, ln_b, need_ctx):
    W = RWKV_WIDTH

    def heads(t):
        return t.reshape(t.shape[:-1] + (RWKV_HEADS, RWKV_HEAD))

    def streams(z):
        z = short_conv(z, conv_w).astype(jnp.float32)
        r, k, v = z[..., :W], z[..., W:2 * W], z[..., 2 * W:3 * W]
        o = 3 * W
        xw = jnp.tanh(z[..., o:o + DECAY_LORA])
        o += DECAY_LORA
        xa = z[..., o:o + AAA_LORA]
        o += AAA_LORA
        xg = z[..., o:]
        kk = heads(k * k_k.astype(jnp.float32))
        kk = kk * lax.rsqrt(jnp.sum(kk * kk, axis=-1, keepdims=True) + 1e-12)
        return r, k, v, xw, xa, xg, kk

    def direction_terms(k, xw, xa, d):
        wl = w0[d].astype(jnp.float32) + xw @ w2[d].astype(jnp.float32)
        decay = jnp.exp(-jnp.exp(-jax.nn.softplus(-wl) - 0.5))
        a = jax.nn.sigmoid(a0[d].astype(jnp.float32) + xa @ a2[d].astype(jnp.float32))
        kt = k * (1.0 + (a - 1.0) * k_a.astype(jnp.float32))
        return heads(decay), heads(a), heads(kt)

    r_l, k_l, v_l, xw_l, xa_l, xg_l, kk_l = streams(z_lat)
    r_c, k_c, v_c, xw_c, xa_c, xg_c, kk_c = streams(z_ctx)
    s0 = jnp.zeros((z_lat.shape[0], RWKV_HEADS, RWKV_HEAD, RWKV_HEAD), jnp.float32)
    ys_l, kts_l, ys_c, kts_c = [], [], [], []
    for d, rev in enumerate((False, True)):
        dec_c, a_c, kt_c = direction_terms(k_c, xw_c, xa_c, d)
        s_c, y_c = rwkv_scan(heads(r_c) if need_ctx else None, dec_c, kt_c, heads(v_c), kk_c, kk_c * a_c, s0, rev)
        dec_l, a_l, kt_l = direction_terms(k_l, xw_l, xa_l, d)
        _, y_l = rwkv_scan(heads(r_l), dec_l, kt_l, heads(v_l), kk_l, kk_l * a_l, s_c, rev)
        ys_l.append(y_l)
        kts_l.append(kt_l)
        if need_ctx:
            ys_c.append(y_c)
            kts_c.append(kt_c)

    def finish(ys, kts, r, v, xg, dtype):
        y = ys[0] + ys[1]
        mu = jnp.mean(y, axis=-1, keepdims=True)
        var = jnp.mean(jnp.square(y - mu), axis=-1, keepdims=True)
        y = (y - mu) * lax.rsqrt(var + GN_EPS) * heads(ln_w.astype(jnp.float32)) + heads(ln_b.astype(jnp.float32))
        bonus = jnp.sum(heads(r) * (kts[0] + kts[1]) * heads(r_k.astype(jnp.float32)), axis=-1, keepdims=True)
        y = y + bonus * heads(v)
        g = jax.nn.sigmoid(xg) @ g2.astype(jnp.float32)
        return (y.reshape(y.shape[:-2] + (W,)) * g).astype(dtype)

    o_lat = finish(ys_l, kts_l, r_l, v_l, xg_l, z_lat.dtype)
    o_ctx = finish(ys_c, kts_c, r_c, v_c, xg_c, z_ctx.dtype) if need_ctx else None
    return o_lat, o_ctx


def merge_branches(gate_logits, o_att, o_s5, o_rwkv, proj_att, proj_s5, proj_rwkv, w_out):
    g = jax.nn.sigmoid(gate_logits.astype(jnp.float32)).astype(o_att.dtype)
    g_att, g_s5, g_rw = jnp.split(g, N_BRANCH, axis=-1)
    merged = g_att * (o_att @ proj_att) + g_s5 * (o_s5 @ proj_s5) + g_rw * (o_rwkv @ proj_rwkv)
    return merged @ w_out


def moe_ffn(h, w_rg, b_rg, w_re, b_re, w_gate, w_up, w_down):
    T, D = h.shape
    hf = h.astype(jnp.float32)
    g_logits = hf @ w_rg.astype(jnp.float32) + b_rg.astype(jnp.float32)
    g_prob = jax.nn.softmax(g_logits, axis=-1)
    g_sel = jnp.argmax(g_logits, axis=-1).astype(jnp.int32)
    g_w = jnp.take_along_axis(g_prob, g_sel[:, None], axis=-1)
    e_logits = (hf @ w_re.astype(jnp.float32) + b_re.astype(jnp.float32)).reshape(T, N_EXPERT_GROUPS, EXPERTS_PER_GROUP)
    e_logits = jnp.take_along_axis(e_logits, g_sel[:, None, None], axis=1)[:, 0]
    e_prob = jax.nn.softmax(e_logits, axis=-1)
    top_p, top_i = lax.top_k(e_prob, TOP_K)
    gate = g_w * top_p / jnp.sum(top_p, axis=-1, keepdims=True)
    expert = g_sel[:, None] * EXPERTS_PER_GROUP + top_i.astype(jnp.int32)

    NK = T * TOP_K
    flat_e = expert.reshape(-1)
    flat_tok = jnp.repeat(jnp.arange(T, dtype=jnp.int32), TOP_K)
    flat_g = gate.reshape(-1)
    order = jnp.argsort(flat_e)
    se = flat_e[order]
    counts = jnp.bincount(flat_e, length=N_EXPERTS).astype(jnp.int32)
    padded = (counts + MOE_BLOCK - 1) // MOE_BLOCK * MOE_BLOCK
    pad_end = jnp.cumsum(padded)
    pad_start = pad_end - padded
    start = jnp.cumsum(counts) - counts
    dest = pad_start[se] + jnp.arange(NK, dtype=jnp.int32) - start[se]
    n_blocks = -(-NK // MOE_BLOCK) + N_EXPERTS
    P = n_blocks * MOE_BLOCK
    slot_tok = jnp.full((P,), T, jnp.int32).at[dest].set(flat_tok[order])
    slot_gate = jnp.zeros((P,), h.dtype).at[dest].set(flat_g[order].astype(h.dtype))
    blk_start = jnp.arange(n_blocks, dtype=jnp.int32) * MOE_BLOCK
    blk_expert = jnp.minimum(jnp.searchsorted(pad_end, blk_start, side='right'), N_EXPERTS - 1).astype(jnp.int32)
    h_pad = jnp.concatenate([h, jnp.zeros((1, D), h.dtype)], axis=0)

    def run_block(args):
        tok, e = args
        xb = h_pad[tok]
        return (jax.nn.silu(xb @ w_gate[e]) * (xb @ w_up[e])) @ w_down[e]

    out = lax.map(run_block, (slot_tok.reshape(n_blocks, MOE_BLOCK), blk_expert))
    y = jnp.zeros((T + 1, D), h.dtype).at[slot_tok].add(out.reshape(P, D) * slot_gate[:, None])
    return y[:T]


def setup_inputs(seed: int = 0) -> dict:
    key = jax.random.key(seed)
    keys = iter(jax.random.split(key, 64))
    f32 = jnp.float32

    def nrm(shape, scale):
        return jax.random.normal(next(keys), shape, f32) * scale

    def unif(shape, lo, hi):
        return jax.random.uniform(next(keys), shape, f32, lo, hi)

    D = D_MODEL
    G, N, CG, W = S5_GROUPS, S5_STATE, S5_GROUP_CH, RWKV_WIDTH
    E, F = N_EXPERTS, D_EXPERT
    return {
        'x': nrm((BATCH, SEQ, D), 1.0),
        'c': nrm((BATCH, D), 1.0),
        'ctx': nrm((BATCH, CTX_LEN, D), 1.0),
        'c_ctx': nrm((D,), 1.0),
        'w_mod': nrm((DEPTH, D, 6 * D), 0.5 * D ** -0.5),
        'b_mod': nrm((DEPTH, 6 * D), 0.02),
        'norm1': 1.0 + nrm((DEPTH, D), 0.05),
        'w_in': nrm((DEPTH, D, IN_WIDTH), D ** -0.5),
        'q_gain': 1.0 + nrm((DEPTH, ATT_HEAD_DIM), 0.05),
        'k_gain': 1.0 + nrm((DEPTH, ATT_HEAD_DIM), 0.05),
        's5_a_re': -0.5 + nrm((DEPTH, 2, G, N), 0.01),
        's5_a_im': math.pi * jnp.arange(N, dtype=f32) + nrm((DEPTH, 2, G, N), 0.01),
        's5_log_dt': unif((DEPTH, 2, G), math.log(1e-3), math.log(1e-1)),
        's5_b_re': nrm((DEPTH, 2, G, N, CG), (2 * CG) ** -0.5),
        's5_b_im': nrm((DEPTH, 2, G, N, CG), (2 * CG) ** -0.5),
        's5_c_re': nrm((DEPTH, 2, G, CG, N), (2 * N) ** -0.5),
        's5_c_im': nrm((DEPTH, 2, G, CG, N), (2 * N) ** -0.5),
        's5_d': nrm((DEPTH, S5_WIDTH), 1.0),
        's5_w_glu': nrm((DEPTH, S5_WIDTH, 2 * S5_WIDTH), S5_WIDTH ** -0.5),
        'rwkv_conv': jnp.array([0.25, 0.5, 0.25], f32)[None, :, None] + nrm((DEPTH, SHORT_CONV, RWKV_STREAM), 0.05),
        'rwkv_w0': unif((DEPTH, 2, W), -6.0, -1.0),
        'rwkv_w2': nrm((DEPTH, 2, DECAY_LORA, W), 0.1 * DECAY_LORA ** -0.5),
        'rwkv_a0': nrm((DEPTH, 2, W), 0.1),
        'rwkv_a2': nrm((DEPTH, 2, AAA_LORA, W), 0.1 * AAA_LORA ** -0.5),
        'rwkv_g2': nrm((DEPTH, GATE_LORA, W), GATE_LORA ** -0.5),
        'rwkv_k_k': 0.85 + nrm((DEPTH, W), 0.05),
        'rwkv_k_a': 1.0 + nrm((DEPTH, W), 0.05),
        'rwkv_r_k': nrm((DEPTH, W), 0.1),
        'rwkv_ln_w': 1.0 + nrm((DEPTH, W), 0.05),
        'rwkv_ln_b': nrm((DEPTH, W), 0.02),
        'proj_att': nrm((DEPTH, ATT_WIDTH, D), ATT_WIDTH ** -0.5),
        'proj_s5': nrm((DEPTH, S5_WIDTH, D), S5_WIDTH ** -0.5),
        'proj_rwkv': nrm((DEPTH, W, D), W ** -0.5),
        'w_out': nrm((DEPTH, D, D), D ** -0.5),
        'norm2': 1.0 + nrm((DEPTH, D), 0.05),
        'router_g_w': nrm((DEPTH, D, N_EXPERT_GROUPS), D ** -0.5),
        'router_g_b': nrm((DEPTH, N_EXPERT_GROUPS), 0.01),
        'router_e_w': nrm((DEPTH, D, E), D ** -0.5),
        'router_e_b': nrm((DEPTH, E), 0.01),
        'exp_w_gate': nrm((DEPTH, E, D, F), D ** -0.5),
        'exp_w_up': nrm((DEPTH, E, D, F), D ** -0.5),
        'exp_w_down': nrm((DEPTH, E, F, D), F ** -0.5),
    }


def reference(x, c, ctx, c_ctx, w_mod, b_mod, norm1, w_in, q_gain, k_gain,
              s5_a_re, s5_a_im, s5_log_dt, s5_b_re, s5_b_im, s5_c_re, s5_c_im, s5_d, s5_w_glu,
              rwkv_conv, rwkv_w0, rwkv_w2, rwkv_a0, rwkv_a2, rwkv_g2, rwkv_k_k, rwkv_k_a, rwkv_r_k,
              rwkv_ln_w, rwkv_ln_b, proj_att, proj_s5, proj_rwkv, w_out, norm2,
              router_g_w, router_g_b, router_e_w, router_e_b, exp_w_gate, exp_w_up, exp_w_down):
    B, L, D = x.shape
    C = ctx.shape[1]
    row, col = grid_positions(L)
    xl, xc = x, ctx
    for l in range(DEPTH):
        need_ctx = l < DEPTH - 1
        mod_lat = (jax.nn.silu(c) @ w_mod[l] + b_mod[l])[:, None, :]
        mod_ctx = jax.nn.silu(c_ctx) @ w_mod[l] + b_mod[l]
        sh1_l, sc1_l, g1_l, sh2_l, sc2_l, g2_l = jnp.split(mod_lat, 6, axis=-1)
        sh1_c, sc1_c, g1_c, sh2_c, sc2_c, g2_c = jnp.split(mod_ctx, 6, axis=-1)

        u_lat = (rmsnorm(xl, norm1[l]) * (1.0 + sc1_l) + sh1_l) @ w_in[l]
        u_ctx = (rmsnorm(xc, norm1[l]) * (1.0 + sc1_c) + sh1_c) @ w_in[l]
        o_att_l, o_att_c = attention_branch(u_lat[..., :OFF_S5], u_ctx[..., :OFF_S5],
                                            q_gain[l], k_gain[l], row, col, need_ctx)
        o_s5_l, o_s5_c = s5_branch(u_lat[..., OFF_S5:OFF_RWKV], u_ctx[..., OFF_S5:OFF_RWKV],
                                   s5_a_re[l], s5_a_im[l], s5_log_dt[l], s5_b_re[l], s5_b_im[l],
                                   s5_c_re[l], s5_c_im[l], s5_d[l], s5_w_glu[l], need_ctx)
        o_rw_l, o_rw_c = rwkv_branch(u_lat[..., OFF_RWKV:OFF_GATE], u_ctx[..., OFF_RWKV:OFF_GATE],
                                     rwkv_conv[l], rwkv_w0[l], rwkv_w2[l], rwkv_a0[l], rwkv_a2[l],
                                     rwkv_g2[l], rwkv_k_k[l], rwkv_k_a[l], rwkv_r_k[l],
                                     rwkv_ln_w[l], rwkv_ln_b[l], need_ctx)
        xl = xl + g1_l * merge_branches(u_lat[..., OFF_GATE:], o_att_l, o_s5_l, o_rw_l,
                                        proj_att[l], proj_s5[l], proj_rwkv[l], w_out[l])

        h2_l = rmsnorm(xl, norm2[l]) * (1.0 + sc2_l) + sh2_l
        if need_ctx:
            xc = xc + g1_c * merge_branches(u_ctx[..., OFF_GATE:], o_att_c, o_s5_c, o_rw_c,
                                            proj_att[l], proj_s5[l], proj_rwkv[l], w_out[l])
            h2_c = rmsnorm(xc, norm2[l]) * (1.0 + sc2_c) + sh2_c
            tokens = jnp.concatenate([h2_l.reshape(B * L, D), h2_c.reshape(B * C, D)], axis=0)
            y = moe_ffn(tokens, router_g_w[l], router_g_b[l], router_e_w[l], router_e_b[l],
                        exp_w_gate[l], exp_w_up[l], exp_w_down[l])
            xl = xl + g2_l * y[:B * L].reshape(B, L, D)
            xc = xc + g2_c * y[B * L:].reshape(B, C, D)
        else:
            y = moe_ffn(h2_l.reshape(B * L, D), router_g_w[l], router_g_b[l], router_e_w[l], router_e_b[l],
                        exp_w_gate[l], exp_w_up[l], exp_w_down[l])
            xl = xl + g2_l * y.reshape(B, L, D)
    return xl
```

```python
import functools

import jax
import jax.numpy as jnp
from jax import lax
from jax.experimental import pallas as pl
from jax.experimental.pallas import tpu as pltpu

F32 = jnp.float32
BF16 = jnp.bfloat16
HIGHEST = lax.Precision.HIGHEST

EPS = 1e-6
GRID_W = 64
ROPE_THETA = 10000.0

HEAD_DIM = 64
ATT_W = 512
KV_W = 256
ATT_IN = ATT_W + 2 * KV_W

S5_W = 256
S5_CG = 16
S5_G = 16
S5_N = 64
S5_CHUNK = 64

RW_W = 256
RW_N = 64
RW_H = 4
LORA = 64
GATE_LORA = 128
RW_STREAM = 3 * RW_W + 2 * LORA + GATE_LORA
GN_EPS = 64e-5
RW_TB = 16
RW_IQ = 16

N_BRANCH = 3
N_GROUPS = 4
EPG = 8
N_EXPERTS = 32
MOE_BLOCK = 256
ROUTE_ROWS = 40

LANES = 128
VMEM_LIMIT = 56 * 1024 * 1024


def _cparams(n_axes, vmem=VMEM_LIMIT):
    return pltpu.CompilerParams(dimension_semantics=("arbitrary",) * n_axes,
                                vmem_limit_bytes=vmem)


def _sigmoid(x):
    return 1.0 / (1.0 + jnp.exp(-x))


def _seg64_sum(x):
    parts = []
    for s in range(x.shape[-1] // LANES):
        xs = x[:, s * LANES:(s + 1) * LANES]
        lo = lax.broadcasted_iota(jnp.int32, xs.shape, 1) < 64
        s_lo = jnp.sum(jnp.where(lo, xs, 0.0), axis=-1, keepdims=True)
        s_hi = jnp.sum(jnp.where(lo, 0.0, xs), axis=-1, keepdims=True)
        parts.append(jnp.where(lo, s_lo, s_hi))
    return parts[0] if len(parts) == 1 else jnp.concatenate(parts, axis=-1)


def _mod_kernel(c_ref, w_ref, b_ref, o_ref):
    c = c_ref[...]
    s = c * _sigmoid(c)
    o_ref[0] = jnp.dot(s, w_ref[0], precision=HIGHEST, preferred_element_type=F32) + b_ref[0]


def modulation(cc, w_mod, b_mod):
    depth, d, n = w_mod.shape
    rows = cc.shape[0]
    nt = n // 4
    return pl.pallas_call(
        _mod_kernel,
        grid=(depth, n // nt),
        in_specs=[pl.BlockSpec((rows, d), lambda l, j: (0, 0)),
                  pl.BlockSpec((1, d, nt), lambda l, j: (l, 0, j)),
                  pl.BlockSpec((1, 1, nt), lambda l, j: (l, 0, j))],
        out_specs=pl.BlockSpec((1, rows, nt), lambda l, j: (l, 0, j)),
        out_shape=jax.ShapeDtypeStruct((depth, rows, n), F32),
        compiler_params=_cparams(2),
        name="modulation",
    )(cc, w_mod, b_mod.reshape(depth, 1, n))


def _inproj_kernel(x_ref, mod_ref, n1_ref, wa_ref, ws_ref, wr_ref, wg_ref,
                   a_ref, s_ref, z_ref, g_ref):
    x = x_ref[0]
    ms = jnp.mean(x * x, axis=-1, keepdims=True)
    y = x * lax.rsqrt(ms + EPS) * n1_ref[...]
    h = (y * (1.0 + mod_ref[0, 0, 1:2, :]) + mod_ref[0, 0, 0:1, :]).astype(BF16)
    a_ref[0] = jnp.dot(h, wa_ref[...], preferred_element_type=F32).astype(BF16)
    s_ref[0] = jnp.dot(h, ws_ref[...], preferred_element_type=F32)
    z_ref[0] = jnp.dot(h, wr_ref[...], preferred_element_type=F32)
    g_ref[0] = jnp.dot(h, wg_ref[...], preferred_element_type=F32).astype(BF16)


def in_projection(xa, modsel, norm1, w_in, tt, t0):
    b, t, d = xa.shape
    ntile = t // tt
    off_s5 = ATT_IN
    off_rw = off_s5 + S5_W
    off_gate = off_rw + RW_STREAM
    wa = w_in[:, :off_s5].astype(BF16)
    ws = w_in[:, off_s5:off_rw].astype(BF16)
    wr = w_in[:, off_rw:off_gate].astype(BF16)
    wg = w_in[:, off_gate:].astype(BF16)
    tok = lambda w: pl.BlockSpec((1, tt, w), lambda i, j: (i, j + t0, 0))
    full = lambda w: pl.BlockSpec((d, w), lambda i, j: (0, 0))
    return pl.pallas_call(
        _inproj_kernel,
        grid=(b, ntile - t0),
        in_specs=[tok(d),
                  pl.BlockSpec((1, 1, 6, d), lambda i, j: (i, jnp.minimum(j + t0, 1), 0, 0)),
                  pl.BlockSpec((1, d), lambda i, j: (0, 0)),
                  full(ATT_IN), full(S5_W), full(RW_STREAM), full(N_BRANCH * d)],
        out_specs=[tok(ATT_IN), tok(S5_W), tok(RW_STREAM), tok(N_BRANCH * d)],
        out_shape=[jax.ShapeDtypeStruct((b, t, ATT_IN), BF16),
                   jax.ShapeDtypeStruct((b, t, S5_W), F32),
                   jax.ShapeDtypeStruct((b, t, RW_STREAM), F32),
                   jax.ShapeDtypeStruct((b, t, N_BRANCH * d), BF16)],
        compiler_params=_cparams(2),
        name="in_projection",
    )(xa, modsel, norm1.reshape(1, d), wa, ws, wr, wg)


def _headnorm(x, gain):
    ms = _seg64_sum(x * x) * (1.0 / HEAD_DIM)
    return x * lax.rsqrt(ms + EPS) * gain


def _rope(x, cos, sin):
    lane = lax.broadcasted_iota(jnp.int32, x.shape, 1)
    first = (lane & 16) == 0
    partner = jnp.where(first, pltpu.roll(x, LANES - 16, 1), pltpu.roll(x, 16, 1))
    return x * cos + partner * sin


def _attend_slab(q, hh, kn, vp):
    lane = lax.broadcasted_iota(jnp.int32, q.shape, 1)
    kv_lanes = (lane >= 64) if hh == 1 else (lane < 64)
    q_sw = pltpu.roll(q, 64, 1)
    srcs = (q, q_sw) if hh == 0 else (q_sw, q)
    outs = []
    for src in srcs:
        qm = jnp.where(kv_lanes, src, 0.0).astype(BF16)
        s = lax.dot_general(qm, kn, (((1,), (1,)), ((), ())), preferred_element_type=F32)
        m = jnp.max(s, axis=-1, keepdims=True)
        e = jnp.exp(s - m)
        l = jnp.sum(e, axis=-1, keepdims=True)
        outs.append(jnp.dot(e.astype(BF16), vp, preferred_element_type=F32) / l)
    if hh == 0:
        placed_a, placed_b = outs[0], pltpu.roll(outs[1], 64, 1)
    else:
        placed_a, placed_b = pltpu.roll(outs[0], 64, 1), outs[1]
    return jnp.where(lane < 64, placed_a, placed_b)


def _attn_kernel(q_ref, kv_ref, cq_ref, sq_ref, ck_ref, sk_ref, qg_ref, kg_ref,
                 o_ref, kn_ref, *, t0, c_len):
    j = pl.program_id(1)
    t_all = kv_ref.shape[1]

    @pl.when(j == 0)
    def _():
        for p in range(2):
            k = kv_ref[0, :, p * LANES:(p + 1) * LANES].astype(F32)
            k = _rope(_headnorm(k, kg_ref[...]), ck_ref[...], sk_ref[...])
            kn_ref[p] = k.astype(BF16)

    def run(n_keys):
        for p in range(2):
            kn = kn_ref[p, 0:n_keys, :]
            vp = kv_ref[0, 0:n_keys, KV_W + p * LANES:KV_W + (p + 1) * LANES]
            for hh in range(2):
                slab = 2 * p + hh
                q = q_ref[0, :, slab * LANES:(slab + 1) * LANES].astype(F32)
                q = _rope(_headnorm(q, qg_ref[...]), cq_ref[...], sq_ref[...]) * (HEAD_DIM ** -0.5)
                o_ref[0, :, slab * LANES:(slab + 1) * LANES] = _attend_slab(q, hh, kn, vp).astype(BF16)

    if t0 == 0:
        pl.when(j == 0)(lambda: run(c_len))
        pl.when(j > 0)(lambda: run(t_all))
    else:
        run(t_all)


def attention(a, cos_t, sin_t, q_gain, k_gain, tt, t0, c_len):
    b, t, _ = a.shape
    ntile = t // tt
    qg = jnp.tile(q_gain, 2).reshape(1, LANES)
    kg = jnp.tile(k_gain, 2).reshape(1, LANES)
    tab_q = pl.BlockSpec((tt, LANES), lambda i, j: (j + t0, 0))
    tab_k = pl.BlockSpec((t, LANES), lambda i, j: (0, 0))
    vec = pl.BlockSpec((1, LANES), lambda i, j: (0, 0))
    return pl.pallas_call(
        functools.partial(_attn_kernel, t0=t0, c_len=c_len),
        grid=(b, ntile - t0),
        in_specs=[pl.BlockSpec((1, tt, ATT_W), lambda i, j: (i, j + t0, 0)),
                  pl.BlockSpec((1, t, ATT_W), lambda i, j: (i, 0, 1)),
                  tab_q, tab_q, tab_k, tab_k, vec, vec],
        out_specs=pl.BlockSpec((1, tt, ATT_W), lambda i, j: (i, j, 0)),
        out_shape=jax.ShapeDtypeStruct((b, t - t0 * tt, ATT_W), BF16),
        scratch_shapes=[pltpu.VMEM((2, t, LANES), BF16)],
        compiler_params=_cparams(2),
        name="attention",
    )(a, a, cos_t, sin_t, cos_t, sin_t, qg, kg)


def rope_tables(c_len, l_len):
    half = HEAD_DIM // 4
    n = jnp.arange(l_len, dtype=jnp.int32)
    inv = ROPE_THETA ** (-jnp.arange(half, dtype=F32) / half)
    ar = (n // GRID_W).astype(F32)[:, None] * inv[None, :]
    ac = (n % GRID_W).astype(F32)[:, None] * inv[None, :]
    cos = jnp.concatenate([jnp.cos(ar), jnp.cos(ar), jnp.cos(ac), jnp.cos(ac)], axis=-1)
    sin = jnp.concatenate([-jnp.sin(ar), jnp.sin(ar), -jnp.sin(ac), jnp.sin(ac)], axis=-1)
    cos = jnp.concatenate([jnp.ones((c_len, HEAD_DIM), F32), cos], axis=0)
    sin = jnp.concatenate([jnp.zeros((c_len, HEAD_DIM), F32), sin], axis=0)
    return jnp.tile(cos, (1, 2)), jnp.tile(sin, (1, 2))


def _s5_kernel(x_ref, mt_ref, qt_ref, pt_ref, ar_ref, ai_ref, y_ref, hloc_ref, hs_ref, *, nb):
    x = x_ref[0, 0]
    hloc_ref[...] = jnp.dot(x, qt_ref[0, 0], preferred_element_type=F32)
    n_chunks = x.shape[0] // nb
    a_r = ar_ref[0, 0]
    a_i = ai_ref[0, 0]

    def body(c, h):
        rows = pl.ds(pl.multiple_of(c * nb, nb), nb)
        hs_ref[rows, :] = h
        return h * a_r + pltpu.roll(h, S5_N, 1) * a_i + hloc_ref[rows, :]

    lax.fori_loop(0, n_chunks, body, jnp.zeros((nb, 2 * S5_N), F32))
    y_ref[0, 0] = (jnp.dot(x, mt_ref[0, 0], preferred_element_type=F32)
                   + jnp.dot(hs_ref[...].astype(BF16), pt_ref[0, 0], preferred_element_type=F32))


def s5_operators(a_re, a_im, log_dt, b_re, b_im, c_re, c_im):
    tc = S5_CHUNK
    dt = jnp.exp(log_dt.astype(F32))[..., None]
    lam_re = a_re.astype(F32)
    lam_im = a_im.astype(F32)
    mag = jnp.exp(lam_re * dt)
    ab_re = mag * jnp.cos(lam_im * dt)
    ab_im = mag * jnp.sin(lam_im * dt)
    den = lam_re * lam_re + lam_im * lam_im
    nr = ab_re - 1.0
    z_re = (nr * lam_re + ab_im * lam_im) / den
    z_im = (ab_im * lam_re - nr * lam_im) / den
    br = b_re.astype(F32)
    bi = b_im.astype(F32)
    bb_re = z_re[..., None] * br - z_im[..., None] * bi
    bb_im = z_re[..., None] * bi + z_im[..., None] * br
    k = jnp.arange(tc + 1, dtype=F32)[:, None, None, None]
    pw_mag = jnp.exp(k * (lam_re * dt)[None])
    pw_re = pw_mag * jnp.cos(k * (lam_im * dt)[None])
    pw_im = pw_mag * jnp.sin(k * (lam_im * dt)[None])
    cr = c_re.astype(F32)
    ci = c_im.astype(F32)
    ca_re = cr[None] * pw_re[:, :, :, None, :] - ci[None] * pw_im[:, :, :, None, :]
    ca_im = cr[None] * pw_im[:, :, :, None, :] + ci[None] * pw_re[:, :, :, None, :]
    kern = (jnp.einsum('kdgcn,dgne->kdgce', ca_re[:tc], bb_re, precision=HIGHEST)
            - jnp.einsum('kdgcn,dgne->kdgce', ca_im[:tc], bb_im, precision=HIGHEST))
    s_idx = jnp.arange(tc)[:, None]
    t_idx = jnp.arange(tc)[None, :]
    lag = t_idx - s_idx
    toe = kern[jnp.clip(lag, 0, tc - 1)]
    toe = jnp.where((lag >= 0)[:, :, None, None, None, None], toe, 0.0)
    mt = jnp.transpose(toe, (2, 3, 0, 5, 1, 4)).reshape(2, S5_G, tc * S5_CG, tc * S5_CG)
    p_re = jnp.transpose(ca_re[1:], (1, 2, 4, 0, 3))
    p_im = jnp.transpose(ca_im[1:], (1, 2, 4, 0, 3))
    pt = jnp.concatenate([p_re, -p_im], axis=2).reshape(2, S5_G, 2 * S5_N, tc * S5_CG)
    rev_re = pw_re[:tc][::-1]
    rev_im = pw_im[:tc][::-1]
    q_re = rev_re[..., None] * bb_re[None] - rev_im[..., None] * bb_im[None]
    q_im = rev_re[..., None] * bb_im[None] + rev_im[..., None] * bb_re[None]
    q_re = jnp.transpose(q_re, (1, 2, 0, 4, 3))
    q_im = jnp.transpose(q_im, (1, 2, 0, 4, 3))
    qt = jnp.concatenate([q_re, q_im], axis=-1).reshape(2, S5_G, tc * S5_CG, 2 * S5_N)
    a_row = jnp.concatenate([pw_re[tc], pw_re[tc]], axis=-1)[:, :, None, :]
    ai_row = jnp.concatenate([-pw_im[tc], pw_im[tc]], axis=-1)[:, :, None, :]
    return mt.astype(BF16), qt.astype(BF16), pt.astype(BF16), a_row, ai_row


def s5_scan(s5u, perm_rev, ops):
    b, t, _ = s5u.shape
    tc = S5_CHUNK
    nch = t // tc
    mt, qt, pt, a_row, ai_row = ops
    both = jnp.stack([s5u, s5u[:, perm_rev]], axis=0)
    x = both.reshape(2, b, nch, tc, S5_G, S5_CG)
    x = jnp.transpose(x, (0, 4, 2, 1, 3, 5)).reshape(2, S5_G, nch * b, tc * S5_CG).astype(BF16)
    r = nch * b
    w = tc * S5_CG
    spec = lambda s0, s1: pl.BlockSpec((1, 1, s0, s1), lambda d, g: (d, g, 0, 0))
    y = pl.pallas_call(
        functools.partial(_s5_kernel, nb=b),
        grid=(2, S5_G),
        in_specs=[spec(r, w), spec(w, w), spec(w, 2 * S5_N), spec(2 * S5_N, w),
                  spec(1, 2 * S5_N), spec(1, 2 * S5_N)],
        out_specs=spec(r, w),
        out_shape=jax.ShapeDtypeStruct((2, S5_G, r, w), F32),
        scratch_shapes=[pltpu.VMEM((r, 2 * S5_N), F32), pltpu.VMEM((r, 2 * S5_N), F32)],
        compiler_params=_cparams(2),
        name="s5_scan",
    )(x, mt, qt, pt, a_row, ai_row)
    y = y.reshape(2, S5_G, nch, b, tc, S5_CG)
    y = jnp.transpose(y, (0, 3, 2, 4, 1, 5)).reshape(2, b, t, S5_W)
    return jnp.stack([y[0], y[1][:, perm_rev]], axis=0)


RW_OUT = 11


def _rwprep_kernel(z_ref, conv_ref, w0_ref, w2_ref, a0_ref, a2_ref, g2_ref, kk_ref, ka_ref, rk_ref,
                   o_ref, *, tt, c_len):
    t_all = z_ref.shape[1]
    r0 = pl.multiple_of(pl.program_id(1) * tt, tt)
    zc = z_ref[0, pl.ds(r0, tt), :]
    row = lax.broadcasted_iota(jnp.int32, zc.shape, 0)
    has_prev = (r0 != 0) & (r0 != c_len)
    has_next = (r0 + tt != c_len) & (r0 + tt != t_all)
    prev_row = z_ref[0, pl.ds(jnp.maximum(r0 - 1, 0), 1), :]
    next_row = z_ref[0, pl.ds(jnp.minimum(r0 + tt, t_all - 1), 1), :]
    prev_row = jnp.where(has_prev, prev_row, 0.0)
    next_row = jnp.where(has_next, next_row, 0.0)
    zp = jnp.where(row == 0, prev_row, pltpu.roll(zc, 1, 0))
    zn = jnp.where(row == tt - 1, next_row, pltpu.roll(zc, tt - 1, 0))
    zz = zp * conv_ref[0:1, :] + zc * conv_ref[1:2, :] + zn * conv_ref[2:3, :]
    r = zz[:, 0:RW_W]
    k = zz[:, RW_W:2 * RW_W]
    v = zz[:, 2 * RW_W:3 * RW_W]
    lora = zz[:, 3 * RW_W:3 * RW_W + 2 * LORA]
    xg = zz[:, 3 * RW_W + 2 * LORA:]
    kk = k * kk_ref[...]
    kk = kk * lax.rsqrt(_seg64_sum(kk * kk) + 1e-12)
    lora_w = jnp.tanh(lora).astype(BF16)
    lora_a = lora.astype(BF16)
    outs = [r, v, kk]
    kt_sum = None
    for d in range(2):
        wl = w0_ref[d:d + 1, :] + jnp.dot(lora_w, w2_ref[d], preferred_element_type=F32)
        sp = jnp.maximum(-wl, 0.0) + jnp.log(1.0 + jnp.exp(-jnp.abs(wl)))
        decay = jnp.exp(-jnp.exp(-sp - 0.5))
        a = _sigmoid(a0_ref[d:d + 1, :] + jnp.dot(lora_a, a2_ref[d], preferred_element_type=F32))
        kt = k * (1.0 + (a - 1.0) * ka_ref[...])
        outs += [decay, kt, kk * a]
        kt_sum = kt if kt_sum is None else kt_sum + kt
    bonus = _seg64_sum(r * kt_sum * rk_ref[...])
    gate = jnp.dot(_sigmoid(xg).astype(BF16), g2_ref[...], preferred_element_type=F32)
    outs = [bonus * v, gate] + outs
    for n, val in enumerate(outs):
        o_ref[0, :, n * RW_W:(n + 1) * RW_W] = val


def rwkv_prepare(z, conv_w, w0, w2, a0, a2, g2, k_k, k_a, r_k, tt, c_len):
    b, t, _ = z.shape
    zeros = jnp.zeros((2, LORA, RW_W), F32)
    w2p = jnp.concatenate([w2, zeros], axis=1).astype(BF16)
    a2p = jnp.concatenate([zeros, a2], axis=1).astype(BF16)
    row = lambda v: v.reshape(1, RW_W)
    c2 = lambda s0, s1: pl.BlockSpec((s0, s1), lambda i, j: (0, 0))
    c3 = lambda s0, s1, s2: pl.BlockSpec((s0, s1, s2), lambda i, j: (0, 0, 0))
    return pl.pallas_call(
        functools.partial(_rwprep_kernel, tt=tt, c_len=c_len),
        grid=(b, t // tt),
        in_specs=[pl.BlockSpec((1, t, RW_STREAM), lambda i, j: (i, 0, 0)),
                  c2(3, RW_STREAM), c2(2, RW_W), c3(2, 2 * LORA, RW_W), c2(2, RW_W),
                  c3(2, 2 * LORA, RW_W), c2(GATE_LORA, RW_W), c2(1, RW_W), c2(1, RW_W), c2(1, RW_W)],
        out_specs=pl.BlockSpec((1, tt, RW_OUT * RW_W), lambda i, j: (i, j, 0)),
        out_shape=jax.ShapeDtypeStruct((b, t, RW_OUT * RW_W), F32),
        compiler_params=_cparams(2),
        name="rwkv_prepare",
    )(z, conv_w, w0, w2p, a0, a2p, g2.astype(BF16), row(k_k), row(k_a), row(r_k))


def _rwscan_kernel(w_ref, kka_ref, kt_ref, r_ref, kkn_ref, v_ref, y_ref, s_ref, sa_ref):
    @pl.when(pl.program_id(0) == 0)
    def _():
        s_ref[...] = jnp.zeros_like(s_ref)
        sa_ref[...] = jnp.zeros_like(sa_ref)

    def step(tt, carry):
        for q in range(RW_N // RW_IQ):
            rows = slice(q * RW_IQ, (q + 1) * RW_IQ)
            sa = sa_ref[rows, :]
            v = v_ref[tt, rows, :]
            acc_sa = jnp.zeros_like(sa)
            acc_y = jnp.zeros_like(sa)
            for j in range(RW_N):
                s_new = (s_ref[j, rows, :] * w_ref[tt, j:j + 1, :]
                         - sa * kka_ref[tt, j:j + 1, :] + v * kt_ref[tt, j:j + 1, :])
                s_ref[j, rows, :] = s_new
                acc_sa = acc_sa + s_new * kkn_ref[tt, j:j + 1, :]
                acc_y = acc_y + s_new * r_ref[tt, j:j + 1, :]
            sa_ref[rows, :] = acc_sa
            y_ref[tt, rows, :] = acc_y
        return carry

    lax.fori_loop(0, w_ref.shape[0], step, 0)


def rwkv_scan(streams, perm_rev):
    b, t, _ = streams.shape
    ni = 2 * b * RW_H

    def pick(n):
        return streams[:, :, n * RW_W:(n + 1) * RW_W]

    def lanes(fwd, rev):
        both = jnp.stack([fwd, rev[:, perm_rev]], axis=0).reshape(2, b, t, RW_H, RW_N)
        return jnp.transpose(both, (2, 4, 0, 1, 3)).reshape(t, RW_N, ni)

    r, v, kk = pick(2), pick(3), pick(4)
    w = lanes(pick(5), pick(8))
    kt = lanes(pick(6), pick(9))
    kka = lanes(pick(7), pick(10))
    r_l = lanes(r, r)
    v_l = lanes(v, v)
    kk_l = lanes(kk, kk)
    kkn = jnp.concatenate([kk_l[1:], jnp.zeros((1, RW_N, ni), F32)], axis=0)
    blk = pl.BlockSpec((RW_TB, RW_N, ni), lambda i: (i, 0, 0))
    y = pl.pallas_call(
        _rwscan_kernel,
        grid=(t // RW_TB,),
        in_specs=[blk] * 6,
        out_specs=blk,
        out_shape=jax.ShapeDtypeStruct((t, RW_N, ni), F32),
        scratch_shapes=[pltpu.VMEM((RW_N, RW_N, ni), F32), pltpu.VMEM((RW_N, ni), F32)],
        compiler_params=_cparams(1),
        name="rwkv_scan",
    )(w, kka, kt, r_l, kkn, v_l)
    y = jnp.transpose(y.reshape(t, RW_N, 2, b, RW_H), (2, 3, 0, 4, 1)).reshape(2, b, t, RW_W)
    return jnp.stack([y[0], y[1][:, perm_rev]], axis=0)


def _merge_kernel(x_ref, mod_ref, gl_ref, oatt_ref, ys5_ref, s5u_ref, yrw_ref, rws_ref,
                  dskip_ref, wglu_ref, lnw_ref, lnb_ref, patt_ref, ps5_ref, prw_ref, wout_ref,
                  n2_ref, wr_ref, br_ref, tri_ref,
                  xo_ref, h2_ref, ri_ref, rg_ref, cnt_ref, base_ref):
    d = x_ref.shape[2]
    first = (pl.program_id(0) == 0) & (pl.program_id(1) == 0)

    @pl.when(first)
    def _():
        base_ref[...] = jnp.zeros_like(base_ref)

    y = ys5_ref[0, 0] + ys5_ref[1, 0] + dskip_ref[...] * s5u_ref[0]
    zg = jnp.dot(jax.nn.gelu(y).astype(BF16), wglu_ref[...], preferred_element_type=F32)
    o_s5 = zg[:, :S5_W] * _sigmoid(zg[:, S5_W:])

    yr = yrw_ref[0, 0] + yrw_ref[1, 0]
    mu = _seg64_sum(yr) * (1.0 / RW_N)
    yc = yr - mu
    var = _seg64_sum(yc * yc) * (1.0 / RW_N)
    yn = yc * lax.rsqrt(var + GN_EPS) * lnw_ref[...] + lnb_ref[...]
    o_rw = (yn + rws_ref[0, :, 0:RW_W]) * rws_ref[0, :, RW_W:2 * RW_W]

    gates = _sigmoid(gl_ref[0].astype(F32))
    merged = (gates[:, 0:d] * jnp.dot(oatt_ref[0], patt_ref[...], preferred_element_type=F32)
              + gates[:, d:2 * d] * jnp.dot(o_s5.astype(BF16), ps5_ref[...], preferred_element_type=F32)
              + gates[:, 2 * d:3 * d] * jnp.dot(o_rw.astype(BF16), prw_ref[...], preferred_element_type=F32))
    upd = jnp.dot(merged.astype(BF16), wout_ref[...], preferred_element_type=F32)
    x_new = x_ref[0] + mod_ref[0, 0, 2:3, :] * upd
    xo_ref[0] = x_new

    ms = jnp.mean(x_new * x_new, axis=-1, keepdims=True)
    h2 = (x_new * lax.rsqrt(ms + EPS) * n2_ref[...]) * (1.0 + mod_ref[0, 0, 4:5, :]) + mod_ref[0, 0, 3:4, :]
    h2_ref[0] = h2

    logits = lax.dot_general(wr_ref[...], h2, (((1,), (1,)), ((), ())),
                             precision=HIGHEST, preferred_element_type=F32) + br_ref[...]
    gl4 = logits[N_EXPERTS:N_EXPERTS + N_GROUPS]
    grow = lax.broadcasted_iota(jnp.int32, gl4.shape, 0)
    gmax = jnp.max(gl4, axis=0, keepdims=True)
    gsel = jnp.min(jnp.where(gl4 == gmax, grow, N_GROUPS), axis=0, keepdims=True)
    g_w = 1.0 / jnp.sum(jnp.exp(gl4 - gmax), axis=0, keepdims=True)
    el = logits[0:N_EXPERTS]
    erow = lax.broadcasted_iota(jnp.int32, el.shape, 0)
    neg = jnp.float32(-1e30)
    cand = jnp.where((erow >> 3) == gsel, el, neg)
    m1 = jnp.max(cand, axis=0, keepdims=True)
    i1 = jnp.min(jnp.where(cand == m1, erow, N_EXPERTS), axis=0, keepdims=True)
    cand2 = jnp.where(erow == i1, neg, cand)
    m2 = jnp.max(cand2, axis=0, keepdims=True)
    i2 = jnp.min(jnp.where(cand2 == m2, erow, N_EXPERTS), axis=0, keepdims=True)
    e21 = jnp.exp(m2 - m1)
    gate1 = g_w / (1.0 + e21)
    gate2 = g_w * e21 / (1.0 + e21)

    oh1 = (erow == i1).astype(F32)
    oh2 = (erow == i2).astype(F32)
    both = oh1 + oh2
    before = jnp.dot(both.astype(BF16), tri_ref[...], preferred_element_type=F32) + base_ref[...]
    rank1 = jnp.sum(oh1 * before, axis=0, keepdims=True)
    rank2 = jnp.sum(oh2 * before, axis=0, keepdims=True)
    base_ref[...] = base_ref[...] + jnp.sum(both, axis=1, keepdims=True)
    cnt_ref[...] = jnp.broadcast_to(base_ref[...], cnt_ref.shape)

    zi = jnp.zeros((4, i1.shape[1]), jnp.int32)
    ri_ref[0, 0:1, :] = i1
    ri_ref[0, 1:2, :] = i2
    ri_ref[0, 2:3, :] = rank1.astype(jnp.int32)
    ri_ref[0, 3:4, :] = rank2.astype(jnp.int32)
    ri_ref[0, 4:8, :] = zi
    rg_ref[0, 0:1, :] = gate1
    rg_ref[0, 1:2, :] = gate2
    rg_ref[0, 2:8, :] = jnp.zeros((6, i1.shape[1]), F32)


def merge_and_route(xa, modsel, gl, o_att, y_s5, s5u, y_rw, rws, p, tt, t0):
    b, t, d = xa.shape
    ntile = t // tt
    nsel = ntile - t0
    tok = lambda w: pl.BlockSpec((1, tt, w), lambda i, j: (i, j + t0, 0))
    sel = lambda w: pl.BlockSpec((1, tt, w), lambda i, j: (i, j, 0))
    tok2 = lambda w: pl.BlockSpec((2, 1, tt, w), lambda i, j: (0, i, j + t0, 0))
    c2 = lambda s0, s1: pl.BlockSpec((s0, s1), lambda i, j: (0, 0))
    wr = jnp.concatenate([p['router_e_w'].T, p['router_g_w'].T,
                          jnp.zeros((ROUTE_ROWS - N_EXPERTS - N_GROUPS, d), F32)], axis=0)
    br = jnp.concatenate([p['router_e_b'], p['router_g_b'],
                          jnp.zeros((ROUTE_ROWS - N_EXPERTS - N_GROUPS,), F32)]).reshape(ROUTE_ROWS, 1)
    tri = (jnp.arange(tt)[:, None] < jnp.arange(tt)[None, :]).astype(BF16)
    route_spec = pl.BlockSpec((1, 8, tt), lambda i, j: (i * nsel + j, 0, 0))
    return pl.pallas_call(
        _merge_kernel,
        grid=(b, nsel),
        in_specs=[tok(d),
                  pl.BlockSpec((1, 1, 6, d), lambda i, j: (i, jnp.minimum(j + t0, 1), 0, 0)),
                  tok(N_BRANCH * d), sel(ATT_W), tok2(S5_W), tok(S5_W), tok2(RW_W),
                  pl.BlockSpec((1, tt, 2 * RW_W), lambda i, j: (i, j + t0, 0)),
                  c2(1, S5_W), c2(S5_W, 2 * S5_W), c2(1, RW_W), c2(1, RW_W),
                  c2(ATT_W, d), c2(S5_W, d), c2(RW_W, d), c2(d, d),
                  c2(1, d), c2(ROUTE_ROWS, d), c2(ROUTE_ROWS, 1), c2(tt, tt)],
        out_specs=[sel(d), sel(d), route_spec, route_spec, c2(N_EXPERTS, LANES)],
        out_shape=[jax.ShapeDtypeStruct((b, nsel * tt, d), F32),
                   jax.ShapeDtypeStruct((b, nsel * tt, d), F32),
                   jax.ShapeDtypeStruct((b * nsel, 8, tt), jnp.int32),
                   jax.ShapeDtypeStruct((b * nsel, 8, tt), F32),
                   jax.ShapeDtypeStruct((N_EXPERTS, LANES), F32)],
        scratch_shapes=[pltpu.VMEM((N_EXPERTS, 1), F32)],
        compiler_params=_cparams(2),
        name="merge_route",
    )(xa, modsel, gl, o_att, y_s5, s5u, y_rw, rws,
      p['s5_d'].reshape(1, S5_W), p['s5_w_glu'].astype(BF16),
      p['rwkv_ln_w'].reshape(1, RW_W), p['rwkv_ln_b'].reshape(1, RW_W),
      p['proj_att'].astype(BF16), p['proj_s5'].astype(BF16), p['proj_rwkv'].astype(BF16),
      p['w_out'].astype(BF16), p['norm2'].reshape(1, d), wr, br, tri)


def _dispatch_kernel(pend_ref, h_ref, dest_ref, hs_in_ref, hs_ref, zero_ref, sem, zsem):
    del hs_in_ref
    tt = h_ref.shape[1]
    first = (pl.program_id(0) == 0) & (pl.program_id(1) == 0)

    def zero_copy(e):
        start = pl.multiple_of(pend_ref[e + 1] - MOE_BLOCK, MOE_BLOCK)
        return pltpu.make_async_copy(zero_ref, hs_ref.at[pl.ds(start, MOE_BLOCK), :], zsem)

    @pl.when(first)
    def _():
        zero_ref[...] = jnp.zeros_like(zero_ref)
        for e in range(N_EXPERTS):
            @pl.when(pend_ref[e + 1] > pend_ref[e])
            def _():
                zero_copy(e).start()
        for e in range(N_EXPERTS):
            @pl.when(pend_ref[e + 1] > pend_ref[e])
            def _():
                zero_copy(e).wait()

    def row_copy(r, k):
        return pltpu.make_async_copy(h_ref.at[0, pl.ds(r, 1), :],
                                     hs_ref.at[pl.ds(dest_ref[0, k, r], 1), :], sem)

    def start(r, c):
        row_copy(r, 0).start()
        row_copy(r, 1).start()
        return c

    def wait(r, c):
        row_copy(r, 0).wait()
        row_copy(r, 1).wait()
        return c

    lax.fori_loop(0, tt, start, 0)
    lax.fori_loop(0, tt, wait, 0)


def moe_dispatch(h2, dest, pad_end0, n_slots, tt):
    b, t, d = h2.shape
    nsel = t // tt
    grid_spec = pltpu.PrefetchScalarGridSpec(
        num_scalar_prefetch=1,
        grid=(b, nsel),
        in_specs=[pl.BlockSpec((1, tt, d), lambda i, j, pe: (i, j, 0)),
                  pl.BlockSpec((1, 2, tt), lambda i, j, pe: (i * nsel + j, 0, 0),
                               memory_space=pltpu.SMEM),
                  pl.BlockSpec(memory_space=pl.ANY)],
        out_specs=pl.BlockSpec(memory_space=pl.ANY),
        scratch_shapes=[pltpu.VMEM((MOE_BLOCK, d), F32),
                        pltpu.SemaphoreType.DMA(()), pltpu.SemaphoreType.DMA(())],
    )
    seed = jnp.zeros((n_slots, d), F32)
    return pl.pallas_call(
        _dispatch_kernel,
        grid_spec=grid_spec,
        out_shape=jax.ShapeDtypeStruct((n_slots, d), F32),
        input_output_aliases={3: 0},
        compiler_params=_cparams(2),
        name="moe_dispatch",
    )(pad_end0, h2, dest, seed)


def _expert_kernel(be_ref, nu_ref, x_ref, wg_ref, wu_ref, wd_ref, o_ref):
    del be_ref
    used = pl.program_id(0) < nu_ref[0]

    @pl.when(used)
    def _():
        x = x_ref[...].astype(BF16)
        a = jnp.dot(x, wg_ref[0], preferred_element_type=F32)
        u = jnp.dot(x, wu_ref[0], preferred_element_type=F32)
        hmid = (a * _sigmoid(a) * u).astype(BF16)
        o_ref[...] = jnp.dot(hmid, wd_ref[0], preferred_element_type=F32)

    @pl.when(jnp.logical_not(used))
    def _():
        o_ref[...] = jnp.zeros_like(o_ref)


def moe_experts(hs, blk_expert, n_used, w_gate, w_up, w_down):
    n_slots, d = hs.shape
    f = w_gate.shape[2]
    nblk = n_slots // MOE_BLOCK
    rows = lambda i, be, nu: (jnp.minimum(i, nu[0] - 1), 0)
    wsel = lambda i, be, nu: (be[jnp.minimum(i, nu[0] - 1)], 0, 0)
    grid_spec = pltpu.PrefetchScalarGridSpec(
        num_scalar_prefetch=2,
        grid=(nblk,),
        in_specs=[pl.BlockSpec((MOE_BLOCK, d), rows),
                  pl.BlockSpec((1, d, f), wsel), pl.BlockSpec((1, d, f), wsel),
                  pl.BlockSpec((1, f, d), wsel)],
        out_specs=pl.BlockSpec((MOE_BLOCK, d), lambda i, be, nu: (i, 0)),
    )
    return pl.pallas_call(
        _expert_kernel,
        grid_spec=grid_spec,
        out_shape=jax.ShapeDtypeStruct((n_slots, d), F32),
        compiler_params=_cparams(1),
        name="moe_experts",
    )(blk_expert, n_used, hs, w_gate.astype(BF16), w_up.astype(BF16), w_down.astype(BF16))


def _combine_kernel(x_ref, mod_ref, dest_ref, gate_ref, ys_ref, o_ref, buf_ref, sem):
    tt = x_ref.shape[1]

    def row_copy(r, k):
        return pltpu.make_async_copy(ys_ref.at[pl.ds(dest_ref[0, k, r], 1), :],
                                     buf_ref.at[k, pl.ds(r, 1), :], sem)

    def start(r, c):
        row_copy(r, 0).start()
        row_copy(r, 1).start()
        return c

    def wait(r, c):
        row_copy(r, 0).wait()
        row_copy(r, 1).wait()
        return c

    lax.fori_loop(0, tt, start, 0)
    lax.fori_loop(0, tt, wait, 0)
    y = gate_ref[0, :, 0:1] * buf_ref[0] + gate_ref[0, :, 1:2] * buf_ref[1]
    o_ref[0] = x_ref[0] + mod_ref[0, 0, 5:6, :] * y


def moe_combine(x_mid, modsel, dest, gates_t, ys, tt, t0):
    b, t, d = x_mid.shape
    nsel = t // tt
    grid_spec = pltpu.PrefetchScalarGridSpec(
        num_scalar_prefetch=0,
        grid=(b, nsel),
        in_specs=[pl.BlockSpec((1, tt, d), lambda i, j: (i, j, 0)),
                  pl.BlockSpec((1, 1, 6, d), lambda i, j: (i, jnp.minimum(j + t0, 1), 0, 0)),
                  pl.BlockSpec((1, 2, tt), lambda i, j: (i * nsel + j, 0, 0), memory_space=pltpu.SMEM),
                  pl.BlockSpec((1, tt, 2), lambda i, j: (i * nsel + j, 0, 0)),
                  pl.BlockSpec(memory_space=pl.ANY)],
        out_specs=pl.BlockSpec((1, tt, d), lambda i, j: (i, j, 0)),
        scratch_shapes=[pltpu.VMEM((2, tt, d), F32), pltpu.SemaphoreType.DMA(())],
    )
    return pl.pallas_call(
        _combine_kernel,
        grid_spec=grid_spec,
        out_shape=jax.ShapeDtypeStruct((b, t, d), F32),
        compiler_params=_cparams(2),
        name="moe_combine",
    )(x_mid, modsel, dest, gates_t, ys)


def moe_layer(x_mid, h2, ri, rg, cnt, modsel, p, tt, t0):
    b, t, d = x_mid.shape
    nsel = t // tt
    n_tok = b * nsel * tt
    n_blocks = -(-(n_tok * 2) // MOE_BLOCK) + N_EXPERTS
    n_slots = n_blocks * MOE_BLOCK
    counts = cnt[:, 0].astype(jnp.int32)
    padded = (counts + MOE_BLOCK - 1) // MOE_BLOCK * MOE_BLOCK
    pad_end = jnp.cumsum(padded)
    pad_start = pad_end - padded
    dest = pad_start[ri[:, 0:2, :]] + ri[:, 2:4, :]
    blk_start = jnp.arange(n_blocks, dtype=jnp.int32) * MOE_BLOCK
    blk_expert = jnp.minimum(jnp.searchsorted(pad_end, blk_start, side='right'),
                             N_EXPERTS - 1).astype(jnp.int32)
    n_used = (pad_end[-1:] // MOE_BLOCK).astype(jnp.int32)
    pad_end0 = jnp.concatenate([jnp.zeros((1,), jnp.int32), pad_end.astype(jnp.int32)])
    hs = moe_dispatch(h2, dest, pad_end0, n_slots, tt)
    ys = moe_experts(hs, blk_expert, n_used, p['exp_w_gate'], p['exp_w_up'], p['exp_w_down'])
    gates_t = jnp.transpose(rg[:, 0:2, :], (0, 2, 1))
    return moe_combine(x_mid, modsel, dest, gates_t, ys, tt, t0)


def kernel(x, c, ctx, c_ctx, w_mod, b_mod, norm1, w_in, q_gain, k_gain, s5_a_re, s5_a_im, s5_log_dt, s5_b_re, s5_b_im, s5_c_re, s5_c_im, s5_d, s5_w_glu, rwkv_conv, rwkv_w0, rwkv_w2, rwkv_a0, rwkv_a2, rwkv_g2, rwkv_k_k, rwkv_k_a, rwkv_r_k, rwkv_ln_w, rwkv_ln_b, proj_att, proj_s5, proj_rwkv, w_out, norm2, router_g_w, router_g_b, router_e_w, router_e_b, exp_w_gate, exp_w_up, exp_w_down):
    b, l_len, d = x.shape
    c_len = ctx.shape[1]
    depth = w_mod.shape[0]
    t = c_len + l_len
    tt = 256 if c_len % 256 == 0 else c_len
    assert l_len % tt == 0 and c_len % S5_CHUNK == 0 and l_len % S5_CHUNK == 0 and t % RW_TB == 0

    rows = -(-(b + 1) // 8) * 8
    cc = jnp.concatenate([c, c_ctx[None], jnp.zeros((rows - b - 1, d), F32)], axis=0)
    mods = modulation(cc, w_mod, b_mod)
    cos_t, sin_t = rope_tables(c_len, l_len)
    perm_rev = jnp.concatenate([jnp.arange(c_len - 1, -1, -1), jnp.arange(t - 1, c_len - 1, -1)])

    xa = jnp.concatenate([ctx, x], axis=1)
    for l in range(depth):
        need_ctx = l < depth - 1
        t0 = 0 if need_ctx else c_len // tt
        mod_lat = mods[l, :b].reshape(b, 1, 6, d)
        mod_ctx = jnp.broadcast_to(mods[l, b].reshape(1, 1, 6, d), (b, 1, 6, d))
        modsel = jnp.concatenate([mod_ctx, mod_lat], axis=1)

        a, s5u, z, gl = in_projection(xa, modsel, norm1[l], w_in[l], tt, 0)
        o_att = attention(a, cos_t, sin_t, q_gain[l], k_gain[l], tt, t0, c_len)
        ops = s5_operators(s5_a_re[l], s5_a_im[l], s5_log_dt[l], s5_b_re[l], s5_b_im[l],
                           s5_c_re[l], s5_c_im[l])
        y_s5 = s5_scan(s5u, perm_rev, ops)
        rws = rwkv_prepare(z, rwkv_conv[l], rwkv_w0[l], rwkv_w2[l], rwkv_a0[l], rwkv_a2[l],
                           rwkv_g2[l], rwkv_k_k[l], rwkv_k_a[l], rwkv_r_k[l], tt, c_len)
        y_rw = rwkv_scan(rws, perm_rev)
        p = dict(s5_d=s5_d[l], s5_w_glu=s5_w_glu[l], rwkv_ln_w=rwkv_ln_w[l], rwkv_ln_b=rwkv_ln_b[l],
                 proj_att=proj_att[l], proj_s5=proj_s5[l], proj_rwkv=proj_rwkv[l], w_out=w_out[l],
                 norm2=norm2[l], router_g_w=router_g_w[l], router_g_b=router_g_b[l],
                 router_e_w=router_e_w[l], router_e_b=router_e_b[l],
                 exp_w_gate=exp_w_gate[l], exp_w_up=exp_w_up[l], exp_w_down=exp_w_down[l])
        x_mid, h2, ri, rg, cnt = merge_and_route(xa, modsel, gl, o_att, y_s5, s5u, y_rw, rws, p, tt, t0)
        xa = moe_layer(x_mid, h2, ri, rg, cnt, modsel, p, tt, t0)
    return xa
```

```python
import functools

import jax
import jax.numpy as jnp
from jax import lax
from jax.experimental import pallas as pl
from jax.experimental.pallas import tpu as pltpu

F32 = jnp.float32
BF16 = jnp.bfloat16
HIGHEST = lax.Precision.HIGHEST

EPS = 1e-6
GRID_W = 64
ROPE_THETA = 10000.0

HEAD_DIM = 64
ATT_W = 512
KV_W = 256
ATT_IN = ATT_W + 2 * KV_W

S5_W = 256
S5_CG = 16
S5_G = 16
S5_N = 64
S5_CHUNK = 16

RW_W = 256
RW_N = 64
RW_H = 4
LORA = 64
GATE_LORA = 128
RW_STREAM = 3 * RW_W + 2 * LORA + GATE_LORA
GN_EPS = 64e-5
RW_TB = 16
RW_IQ = 16

N_BRANCH = 3
N_GROUPS = 4
EPG = 8
N_EXPERTS = 32
MOE_BLOCK = 256
ROUTE_ROWS = 40

LANES = 128
VMEM_LIMIT = 56 * 1024 * 1024


def _cparams(n_axes, vmem=VMEM_LIMIT):
    return pltpu.CompilerParams(dimension_semantics=("arbitrary",) * n_axes,
                                vmem_limit_bytes=vmem)


def _sigmoid(x):
    return 1.0 / (1.0 + jnp.exp(-x))


def _seg64_sum(x):
    parts = []
    for s in range(x.shape[-1] // LANES):
        xs = x[:, s * LANES:(s + 1) * LANES]
        lo = lax.broadcasted_iota(jnp.int32, xs.shape, 1) < 64
        s_lo = jnp.sum(jnp.where(lo, xs, 0.0), axis=-1, keepdims=True)
        s_hi = jnp.sum(jnp.where(lo, 0.0, xs), axis=-1, keepdims=True)
        parts.append(jnp.where(lo, s_lo, s_hi))
    return parts[0] if len(parts) == 1 else jnp.concatenate(parts, axis=-1)


def _mod_kernel(c_ref, w_ref, b_ref, o_ref):
    c = c_ref[...]
    s = c * _sigmoid(c)
    o_ref[0] = jnp.dot(s, w_ref[0], precision=HIGHEST, preferred_element_type=F32) + b_ref[0]


def modulation(cc, w_mod, b_mod):
    depth, d, n = w_mod.shape
    rows = cc.shape[0]
    nt = n // 4
    return pl.pallas_call(
        _mod_kernel,
        grid=(depth, n // nt),
        in_specs=[pl.BlockSpec((rows, d), lambda l, j: (0, 0)),
                  pl.BlockSpec((1, d, nt), lambda l, j: (l, 0, j)),
                  pl.BlockSpec((1, 1, nt), lambda l, j: (l, 0, j))],
        out_specs=pl.BlockSpec((1, rows, nt), lambda l, j: (l, 0, j)),
        out_shape=jax.ShapeDtypeStruct((depth, rows, n), F32),
        compiler_params=_cparams(2),
        name="modulation",
    )(cc, w_mod, b_mod.reshape(depth, 1, n))


def _inproj_kernel(x_ref, mod_ref, n1_ref, wa_ref, ws_ref, wr_ref, wg_ref,
                   a_ref, s_ref, z_ref, g_ref):
    x = x_ref[0]
    ms = jnp.mean(x * x, axis=-1, keepdims=True)
    y = x * lax.rsqrt(ms + EPS) * n1_ref[...]
    h = (y * (1.0 + mod_ref[0, 0, 1:2, :]) + mod_ref[0, 0, 0:1, :]).astype(BF16)
    a_ref[0] = jnp.dot(h, wa_ref[...], preferred_element_type=F32).astype(BF16)
    s_ref[0] = jnp.dot(h, ws_ref[...], preferred_element_type=F32)
    z_ref[0] = jnp.dot(h, wr_ref[...], preferred_element_type=F32)
    g_ref[0] = jnp.dot(h, wg_ref[...], preferred_element_type=F32).astype(BF16)


def in_projection(xa, modsel, norm1, w_in, tt, t0):
    b, t, d = xa.shape
    ntile = t // tt
    off_s5 = ATT_IN
    off_rw = off_s5 + S5_W
    off_gate = off_rw + RW_STREAM
    wa = w_in[:, :off_s5].astype(BF16)
    ws = w_in[:, off_s5:off_rw].astype(BF16)
    wr = w_in[:, off_rw:off_gate].astype(BF16)
    wg = w_in[:, off_gate:].astype(BF16)
    tok = lambda w: pl.BlockSpec((1, tt, w), lambda i, j: (i, j + t0, 0))
    full = lambda w: pl.BlockSpec((d, w), lambda i, j: (0, 0))
    return pl.pallas_call(
        _inproj_kernel,
        grid=(b, ntile - t0),
        in_specs=[tok(d),
                  pl.BlockSpec((1, 1, 6, d), lambda i, j: (i, jnp.minimum(j + t0, 1), 0, 0)),
                  pl.BlockSpec((1, d), lambda i, j: (0, 0)),
                  full(ATT_IN), full(S5_W), full(RW_STREAM), full(N_BRANCH * d)],
        out_specs=[tok(ATT_IN), tok(S5_W), tok(RW_STREAM), tok(N_BRANCH * d)],
        out_shape=[jax.ShapeDtypeStruct((b, t, ATT_IN), BF16),
                   jax.ShapeDtypeStruct((b, t, S5_W), F32),
                   jax.ShapeDtypeStruct((b, t, RW_STREAM), F32),
                   jax.ShapeDtypeStruct((b, t, N_BRANCH * d), BF16)],
        compiler_params=_cparams(2),
        name="in_projection",
    )(xa, modsel, norm1.reshape(1, d), wa, ws, wr, wg)


def _headnorm(x, gain):
    ms = _seg64_sum(x * x) * (1.0 / HEAD_DIM)
    return x * lax.rsqrt(ms + EPS) * gain


def _rope(x, cos, sin):
    lane = lax.broadcasted_iota(jnp.int32, x.shape, 1)
    first = (lane & 16) == 0
    partner = jnp.where(first, pltpu.roll(x, LANES - 16, 1), pltpu.roll(x, 16, 1))
    return x * cos + partner * sin


def _attend_slab(q, hh, kn, vp):
    lane = lax.broadcasted_iota(jnp.int32, q.shape, 1)
    kv_lanes = (lane >= 64) if hh == 1 else (lane < 64)
    q_sw = pltpu.roll(q, 64, 1)
    srcs = (q, q_sw) if hh == 0 else (q_sw, q)
    outs = []
    for src in srcs:
        qm = jnp.where(kv_lanes, src, 0.0).astype(BF16)
        s = lax.dot_general(qm, kn, (((1,), (1,)), ((), ())), preferred_element_type=F32)
        m = jnp.max(s, axis=-1, keepdims=True)
        e = jnp.exp(s - m)
        l = jnp.sum(e, axis=-1, keepdims=True)
        outs.append(jnp.dot(e.astype(BF16), vp, preferred_element_type=F32) / l)
    if hh == 0:
        placed_a, placed_b = outs[0], pltpu.roll(outs[1], 64, 1)
    else:
        placed_a, placed_b = pltpu.roll(outs[0], 64, 1), outs[1]
    return jnp.where(lane < 64, placed_a, placed_b)


def _attn_kernel(q_ref, kv_ref, cq_ref, sq_ref, ck_ref, sk_ref, qg_ref, kg_ref,
                 o_ref, kn_ref, *, t0, c_len):
    j = pl.program_id(1)
    t_all = kv_ref.shape[1]

    @pl.when(j == 0)
    def _():
        for p in range(2):
            k = kv_ref[0, :, p * LANES:(p + 1) * LANES].astype(F32)
            k = _rope(_headnorm(k, kg_ref[...]), ck_ref[...], sk_ref[...])
            kn_ref[p] = k.astype(BF16)

    def run(n_keys):
        for p in range(2):
            kn = kn_ref[p, 0:n_keys, :]
            vp = kv_ref[0, 0:n_keys, KV_W + p * LANES:KV_W + (p + 1) * LANES]
            for hh in range(2):
                slab = 2 * p + hh
                q = q_ref[0, :, slab * LANES:(slab + 1) * LANES].astype(F32)
                q = _rope(_headnorm(q, qg_ref[...]), cq_ref[...], sq_ref[...]) * (HEAD_DIM ** -0.5)
                o_ref[0, :, slab * LANES:(slab + 1) * LANES] = _attend_slab(q, hh, kn, vp).astype(BF16)

    if t0 == 0:
        pl.when(j == 0)(lambda: run(c_len))
        pl.when(j > 0)(lambda: run(t_all))
    else:
        run(t_all)


def attention(a, cos_t, sin_t, q_gain, k_gain, tt, t0, c_len):
    b, t, _ = a.shape
    ntile = t // tt
    qg = jnp.tile(q_gain, 2).reshape(1, LANES)
    kg = jnp.tile(k_gain, 2).reshape(1, LANES)
    tab_q = pl.BlockSpec((tt, LANES), lambda i, j: (j + t0, 0))
    tab_k = pl.BlockSpec((t, LANES), lambda i, j: (0, 0))
    vec = pl.BlockSpec((1, LANES), lambda i, j: (0, 0))
    return pl.pallas_call(
        functools.partial(_attn_kernel, t0=t0, c_len=c_len),
        grid=(b, ntile - t0),
        in_specs=[pl.BlockSpec((1, tt, ATT_W), lambda i, j: (i, j + t0, 0)),
                  pl.BlockSpec((1, t, ATT_W), lambda i, j: (i, 0, 1)),
                  tab_q, tab_q, tab_k, tab_k, vec, vec],
        out_specs=pl.BlockSpec((1, tt, ATT_W), lambda i, j: (i, j, 0)),
        out_shape=jax.ShapeDtypeStruct((b, t - t0 * tt, ATT_W), BF16),
        scratch_shapes=[pltpu.VMEM((2, t, LANES), BF16)],
        compiler_params=_cparams(2),
        name="attention",
    )(a, a, cos_t, sin_t, cos_t, sin_t, qg, kg)


def rope_tables(c_len, l_len):
    half = HEAD_DIM // 4
    n = jnp.arange(l_len, dtype=jnp.int32)
    inv = ROPE_THETA ** (-jnp.arange(half, dtype=F32) / half)
    ar = (n // GRID_W).astype(F32)[:, None] * inv[None, :]
    ac = (n % GRID_W).astype(F32)[:, None] * inv[None, :]
    cos = jnp.concatenate([jnp.cos(ar), jnp.cos(ar), jnp.cos(ac), jnp.cos(ac)], axis=-1)
    sin = jnp.concatenate([-jnp.sin(ar), jnp.sin(ar), -jnp.sin(ac), jnp.sin(ac)], axis=-1)
    cos = jnp.concatenate([jnp.ones((c_len, HEAD_DIM), F32), cos], axis=0)
    sin = jnp.concatenate([jnp.zeros((c_len, HEAD_DIM), F32), sin], axis=0)
    return jnp.tile(cos, (1, 2)), jnp.tile(sin, (1, 2))


def _s5_kernel(x_ref, mt_ref, qt_ref, pt_ref, ar_ref, ai_ref, y_ref, hloc_ref, hs_ref, *, nb, n_ctx):
    d = pl.program_id(0)
    x = x_ref[0]
    hloc_ref[...] = jnp.dot(x, qt_ref[0, 0], preferred_element_type=F32)
    n_chunks = x.shape[0] // nb
    a_r = ar_ref[0, 0]
    a_i = ai_ref[0, 0]

    def body(k, h):
        k_rev = jnp.where(k < n_ctx, n_ctx - 1 - k, n_chunks - 1 - (k - n_ctx))
        c = jnp.where(d == 0, k, k_rev)
        rows = pl.ds(pl.multiple_of(c * nb, nb), nb)
        hs_ref[rows, :] = h
        return h * a_r + pltpu.roll(h, S5_N, 1) * a_i + hloc_ref[rows, :]

    lax.fori_loop(0, n_chunks, body, jnp.zeros((nb, 2 * S5_N), F32))
    y_ref[0, 0] = (jnp.dot(x, mt_ref[0, 0], preferred_element_type=F32)
                   + jnp.dot(hs_ref[...].astype(BF16), pt_ref[0, 0], preferred_element_type=F32))


def s5_operators(a_re, a_im, log_dt, b_re, b_im, c_re, c_im):
    tc = S5_CHUNK
    dt = jnp.exp(log_dt.astype(F32))[..., None]
    lam_re = a_re.astype(F32)
    lam_im = a_im.astype(F32)
    mag = jnp.exp(lam_re * dt)
    ab_re = mag * jnp.cos(lam_im * dt)
    ab_im = mag * jnp.sin(lam_im * dt)
    den = lam_re * lam_re + lam_im * lam_im
    nr = ab_re - 1.0
    z_re = (nr * lam_re + ab_im * lam_im) / den
    z_im = (ab_im * lam_re - nr * lam_im) / den
    br = b_re.astype(F32)
    bi = b_im.astype(F32)
    bb_re = z_re[..., None] * br - z_im[..., None] * bi
    bb_im = z_re[..., None] * bi + z_im[..., None] * br
    k = jnp.arange(tc + 1, dtype=F32)[:, None, None, None]
    pw_mag = jnp.exp(k * (lam_re * dt)[None])
    pw_re = pw_mag * jnp.cos(k * (lam_im * dt)[None])
    pw_im = pw_mag * jnp.sin(k * (lam_im * dt)[None])
    cr = c_re.astype(F32)
    ci = c_im.astype(F32)
    ca_re = cr[None] * pw_re[:, :, :, None, :] - ci[None] * pw_im[:, :, :, None, :]
    ca_im = cr[None] * pw_im[:, :, :, None, :] + ci[None] * pw_re[:, :, :, None, :]
    kern = (jnp.einsum('kdgcn,dgne->kdgce', ca_re[:tc], bb_re, precision=HIGHEST)
            - jnp.einsum('kdgcn,dgne->kdgce', ca_im[:tc], bb_im, precision=HIGHEST))
    s_idx = jnp.arange(tc)[:, None]
    t_idx = jnp.arange(tc)[None, :]
    mts, pts, qts = [], [], []
    for d in range(2):
        lag = (t_idx - s_idx) if d == 0 else (s_idx - t_idx)
        toe = kern[:, d][jnp.clip(lag, 0, tc - 1)]
        toe = jnp.where((lag >= 0)[:, :, None, None, None], toe, 0.0)
        mts.append(jnp.transpose(toe, (2, 0, 4, 1, 3)).reshape(S5_G, tc * S5_CG, tc * S5_CG))
        steps = slice(1, tc + 1) if d == 0 else slice(tc, 0, -1)
        p_re = jnp.transpose(ca_re[steps, d], (1, 3, 0, 2))
        p_im = jnp.transpose(ca_im[steps, d], (1, 3, 0, 2))
        pts.append(jnp.concatenate([p_re, -p_im], axis=1).reshape(S5_G, 2 * S5_N, tc * S5_CG))
        order = slice(tc - 1, None, -1) if d == 0 else slice(0, tc)
        rev_re = pw_re[order, d]
        rev_im = pw_im[order, d]
        q_re = rev_re[..., None] * bb_re[d][None] - rev_im[..., None] * bb_im[d][None]
        q_im = rev_re[..., None] * bb_im[d][None] + rev_im[..., None] * bb_re[d][None]
        q_re = jnp.transpose(q_re, (1, 0, 3, 2))
        q_im = jnp.transpose(q_im, (1, 0, 3, 2))
        qts.append(jnp.concatenate([q_re, q_im], axis=-1).reshape(S5_G, tc * S5_CG, 2 * S5_N))
    a_row = jnp.concatenate([pw_re[tc], pw_re[tc]], axis=-1)[:, :, None, :]
    ai_row = jnp.concatenate([-pw_im[tc], pw_im[tc]], axis=-1)[:, :, None, :]
    return (jnp.stack(mts).astype(BF16), jnp.stack(qts).astype(BF16), jnp.stack(pts).astype(BF16),
            a_row, ai_row)


def s5_scan(s5u, ops, c_len):
    b, t, _ = s5u.shape
    tc = S5_CHUNK
    nch = t // tc
    mt, qt, pt, a_row, ai_row = ops
    x = s5u.reshape(b, nch, tc, S5_G, S5_CG)
    x = jnp.transpose(x, (3, 1, 0, 2, 4)).reshape(S5_G, nch * b, tc * S5_CG).astype(BF16)
    r = nch * b
    w = tc * S5_CG
    spec = lambda s0, s1: pl.BlockSpec((1, 1, s0, s1), lambda d, g: (d, g, 0, 0))
    y = pl.pallas_call(
        functools.partial(_s5_kernel, nb=b, n_ctx=c_len // tc),
        grid=(2, S5_G),
        in_specs=[pl.BlockSpec((1, r, w), lambda d, g: (g, 0, 0)),
                  spec(w, w), spec(w, 2 * S5_N), spec(2 * S5_N, w),
                  spec(1, 2 * S5_N), spec(1, 2 * S5_N)],
        out_specs=spec(r, w),
        out_shape=jax.ShapeDtypeStruct((2, S5_G, r, w), F32),
        scratch_shapes=[pltpu.VMEM((r, 2 * S5_N), F32), pltpu.VMEM((r, 2 * S5_N), F32)],
        compiler_params=_cparams(2),
        name="s5_scan",
    )(x, mt, qt, pt, a_row, ai_row)
    y = y.reshape(2, S5_G, nch, b, tc, S5_CG)
    return jnp.transpose(y, (0, 3, 2, 4, 1, 5)).reshape(2, b, t, S5_W)


RW_STREAMS = 9


def _rwprep_kernel(z_ref, conv_ref, w0_ref, w2_ref, a0_ref, a2_ref, g2_ref, kk_ref, ka_ref, rk_ref,
                   fin_ref, *stream_refs, tt, c_len):
    t_all = z_ref.shape[1]
    r0 = pl.multiple_of(pl.program_id(1) * tt, tt)
    zc = z_ref[0, pl.ds(r0, tt), :]
    row = lax.broadcasted_iota(jnp.int32, zc.shape, 0)
    has_prev = (r0 != 0) & (r0 != c_len)
    has_next = (r0 + tt != c_len) & (r0 + tt != t_all)
    prev_row = z_ref[0, pl.ds(jnp.maximum(r0 - 1, 0), 1), :]
    next_row = z_ref[0, pl.ds(jnp.minimum(r0 + tt, t_all - 1), 1), :]
    prev_row = jnp.where(has_prev, prev_row, 0.0)
    next_row = jnp.where(has_next, next_row, 0.0)
    zp = jnp.where(row == 0, prev_row, pltpu.roll(zc, 1, 0))
    zn = jnp.where(row == tt - 1, next_row, pltpu.roll(zc, tt - 1, 0))
    zz = zp * conv_ref[0:1, :] + zc * conv_ref[1:2, :] + zn * conv_ref[2:3, :]
    r = zz[:, 0:RW_W]
    k = zz[:, RW_W:2 * RW_W]
    v = zz[:, 2 * RW_W:3 * RW_W]
    lora = zz[:, 3 * RW_W:3 * RW_W + 2 * LORA]
    xg = zz[:, 3 * RW_W + 2 * LORA:]
    kk = k * kk_ref[...]
    kk = kk * lax.rsqrt(_seg64_sum(kk * kk) + 1e-12)
    lora_w = jnp.tanh(lora).astype(BF16)
    lora_a = lora.astype(BF16)
    outs = [r, v, kk]
    kt_sum = None
    for d in range(2):
        wl = w0_ref[d:d + 1, :] + jnp.dot(lora_w, w2_ref[d], preferred_element_type=F32)
        sp = jnp.maximum(-wl, 0.0) + jnp.log(1.0 + jnp.exp(-jnp.abs(wl)))
        decay = jnp.exp(-jnp.exp(-sp - 0.5))
        a = _sigmoid(a0_ref[d:d + 1, :] + jnp.dot(lora_a, a2_ref[d], preferred_element_type=F32))
        kt = k * (1.0 + (a - 1.0) * ka_ref[...])
        outs += [decay, kk * a, kt]
        kt_sum = kt if kt_sum is None else kt_sum + kt
    bonus = _seg64_sum(r * kt_sum * rk_ref[...])
    gate = jnp.dot(_sigmoid(xg).astype(BF16), g2_ref[...], preferred_element_type=F32)
    fin_ref[0, :, 0:RW_W] = bonus * v
    fin_ref[0, :, RW_W:2 * RW_W] = gate
    for ref, val in zip(stream_refs, outs):
        ref[...] = val


def rwkv_prepare(z, conv_w, w0, w2, a0, a2, g2, k_k, k_a, r_k, tt, c_len):
    b, t, _ = z.shape
    zeros = jnp.zeros((2, LORA, RW_W), F32)
    w2p = jnp.concatenate([w2, zeros], axis=1).astype(BF16)
    a2p = jnp.concatenate([zeros, a2], axis=1).astype(BF16)
    row = lambda v: v.reshape(1, RW_W)
    c2 = lambda s0, s1: pl.BlockSpec((s0, s1), lambda i, j: (0, 0))
    c3 = lambda s0, s1, s2: pl.BlockSpec((s0, s1, s2), lambda i, j: (0, 0, 0))
    outs = pl.pallas_call(
        functools.partial(_rwprep_kernel, tt=tt, c_len=c_len),
        grid=(b, t // tt),
        in_specs=[pl.BlockSpec((1, t, RW_STREAM), lambda i, j: (i, 0, 0)),
                  c2(3, RW_STREAM), c2(2, RW_W), c3(2, 2 * LORA, RW_W), c2(2, RW_W),
                  c3(2, 2 * LORA, RW_W), c2(GATE_LORA, RW_W), c2(1, RW_W), c2(1, RW_W), c2(1, RW_W)],
        out_specs=[pl.BlockSpec((1, tt, 2 * RW_W), lambda i, j: (i, j, 0))]
        + [pl.BlockSpec((tt, RW_W), lambda i, j: (j, i))] * RW_STREAMS,
        out_shape=[jax.ShapeDtypeStruct((b, t, 2 * RW_W), F32)]
        + [jax.ShapeDtypeStruct((t, b * RW_W), F32)] * RW_STREAMS,
        compiler_params=_cparams(2),
        name="rwkv_prepare",
    )(z, conv_w, w0, w2p, a0, a2p, g2.astype(BF16), row(k_k), row(k_a), row(r_k))
    return outs[0], outs[1:]


RW_DIR_INPUTS = 7


def _key_row(ref, lead, j):
    return ref[(*lead, pl.ds(j, RW_IQ, stride=0), slice(None))]


def _rwscan_kernel(*refs):
    ins = refs[:2 * RW_DIR_INPUTS]
    y_refs = refs[2 * RW_DIR_INPUTS:2 * RW_DIR_INPUTS + 2]
    s_ref, sa_ref, kkx_ref = refs[2 * RW_DIR_INPUTS + 2:]
    tb = y_refs[0].shape[0]

    @pl.when(pl.program_id(0) == 0)
    def _():
        s_ref[...] = jnp.zeros_like(s_ref)
        sa_ref[...] = jnp.zeros_like(sa_ref)

    kkx_ref[0, 0:tb] = ins[5][...]
    kkx_ref[0, tb:tb + 1] = ins[6][0:1]
    kkx_ref[1, 1:tb + 1] = ins[RW_DIR_INPUTS + 5][...]
    kkx_ref[1, 0:1] = ins[RW_DIR_INPUTS + 6][tb - 1:tb]

    def step(tt, carry):
        for d in range(2):
            w_ref, kka_ref, kt_ref, r_ref, v_ref = ins[RW_DIR_INPUTS * d:RW_DIR_INPUTS * d + 5]
            row = tt if d == 0 else tb - 1 - tt
            nxt = tt + 1 if d == 0 else tb - 1 - tt

            def bc(ref, j):
                return _key_row(ref, (row,), j)

            for q in range(RW_N // RW_IQ):
                rows = slice(q * RW_IQ, (q + 1) * RW_IQ)
                sa = sa_ref[d, rows, :]
                v = v_ref[row, rows, :]
                acc_sa = jnp.zeros_like(sa)
                acc_y = jnp.zeros_like(sa)
                for j in range(RW_N):
                    s_new = s_ref[j, d, rows, :] * bc(w_ref, j) - sa * bc(kka_ref, j) + v * bc(kt_ref, j)
                    s_ref[j, d, rows, :] = s_new
                    acc_sa = acc_sa + s_new * _key_row(kkx_ref, (d, nxt), j)
                    acc_y = acc_y + s_new * bc(r_ref, j)
                sa_ref[d, rows, :] = acc_sa
                y_refs[d][row, rows, :] = acc_y
        return carry

    lax.fori_loop(0, tb, step, 0)


def rwkv_scan(streams, b, c_len):
    t = streams[0].shape[0]
    ni = b * RW_H
    n_blk = t // RW_TB
    n_ctx = c_len // RW_TB
    r, v, kk, w0, kka0, kt0, w1, kka1, kt1 = [
        jnp.swapaxes(s.reshape(t, ni, RW_N), 1, 2) for s in streams]

    fwd = lambda i: i
    rev = lambda i: jnp.where(i < n_ctx, n_ctx - 1 - i, n_blk - 1 - (i - n_ctx))
    after = lambda i: jnp.minimum(i + 1, n_blk - 1)
    spec = lambda f: pl.BlockSpec((RW_TB, RW_N, ni), lambda i: (f(i), 0, 0))
    dir_specs = lambda f: [spec(f)] * 6 + [spec(lambda i: f(after(i)))]
    y0, y1 = pl.pallas_call(
        _rwscan_kernel,
        grid=(n_blk,),
        in_specs=dir_specs(fwd) + dir_specs(rev),
        out_specs=[spec(fwd), spec(rev)],
        out_shape=[jax.ShapeDtypeStruct((t, RW_N, ni), F32)] * 2,
        scratch_shapes=[pltpu.VMEM((RW_N, 2, RW_N, ni), F32), pltpu.VMEM((2, RW_N, ni), F32),
                        pltpu.VMEM((2, RW_TB + 1, RW_N, ni), F32)],
        compiler_params=_cparams(1),
        name="rwkv_scan",
    )(w0, kka0, kt0, r, v, kk, kk, w1, kka1, kt1, r, v, kk, kk)
    back = lambda y: jnp.swapaxes(y, 1, 2).reshape(t, b * RW_W)
    return back(y0), back(y1)


def _merge_kernel(x_ref, mod_ref, gl_ref, oatt_ref, ys5_ref, s5u_ref, yrw0_ref, yrw1_ref, rws_ref,
                  dskip_ref, wglu_ref, lnw_ref, lnb_ref, patt_ref, ps5_ref, prw_ref, wout_ref,
                  n2_ref, wr_ref, br_ref, tri_ref,
                  xo_ref, h2_ref, ri_ref, rg_ref, cnt_ref, base_ref):
    d = x_ref.shape[2]
    first = (pl.program_id(0) == 0) & (pl.program_id(1) == 0)

    @pl.when(first)
    def _():
        base_ref[...] = jnp.zeros_like(base_ref)

    y = ys5_ref[0, 0] + ys5_ref[1, 0] + dskip_ref[...] * s5u_ref[0]
    zg = jnp.dot(jax.nn.gelu(y).astype(BF16), wglu_ref[...], preferred_element_type=F32)
    o_s5 = zg[:, :S5_W] * _sigmoid(zg[:, S5_W:])

    yr = yrw0_ref[...] + yrw1_ref[...]
    mu = _seg64_sum(yr) * (1.0 / RW_N)
    yc = yr - mu
    var = _seg64_sum(yc * yc) * (1.0 / RW_N)
    yn = yc * lax.rsqrt(var + GN_EPS) * lnw_ref[...] + lnb_ref[...]
    o_rw = (yn + rws_ref[0, :, 0:RW_W]) * rws_ref[0, :, RW_W:2 * RW_W]

    gates = _sigmoid(gl_ref[0].astype(F32))
    merged = (gates[:, 0:d] * jnp.dot(oatt_ref[0], patt_ref[...], preferred_element_type=F32)
              + gates[:, d:2 * d] * jnp.dot(o_s5.astype(BF16), ps5_ref[...], preferred_element_type=F32)
              + gates[:, 2 * d:3 * d] * jnp.dot(o_rw.astype(BF16), prw_ref[...], preferred_element_type=F32))
    upd = jnp.dot(merged.astype(BF16), wout_ref[...], preferred_element_type=F32)
    x_new = x_ref[0] + mod_ref[0, 0, 2:3, :] * upd
    xo_ref[0] = x_new

    ms = jnp.mean(x_new * x_new, axis=-1, keepdims=True)
    h2 = (x_new * lax.rsqrt(ms + EPS) * n2_ref[...]) * (1.0 + mod_ref[0, 0, 4:5, :]) + mod_ref[0, 0, 3:4, :]
    h2_ref[0] = h2

    logits = lax.dot_general(wr_ref[...], h2, (((1,), (1,)), ((), ())),
                             precision=HIGHEST, preferred_element_type=F32) + br_ref[...]
    gl4 = logits[N_EXPERTS:N_EXPERTS + N_GROUPS]
    grow = lax.broadcasted_iota(jnp.int32, gl4.shape, 0)
    gmax = jnp.max(gl4, axis=0, keepdims=True)
    gsel = jnp.min(jnp.where(gl4 == gmax, grow, N_GROUPS), axis=0, keepdims=True)
    g_w = 1.0 / jnp.sum(jnp.exp(gl4 - gmax), axis=0, keepdims=True)
    el = logits[0:N_EXPERTS]
    erow = lax.broadcasted_iota(jnp.int32, el.shape, 0)
    neg = jnp.float32(-1e30)
    cand = jnp.where((erow >> 3) == gsel, el, neg)
    m1 = jnp.max(cand, axis=0, keepdims=True)
    i1 = jnp.min(jnp.where(cand == m1, erow, N_EXPERTS), axis=0, keepdims=True)
    cand2 = jnp.where(erow == i1, neg, cand)
    m2 = jnp.max(cand2, axis=0, keepdims=True)
    i2 = jnp.min(jnp.where(cand2 == m2, erow, N_EXPERTS), axis=0, keepdims=True)
    e21 = jnp.exp(m2 - m1)
    gate1 = g_w / (1.0 + e21)
    gate2 = g_w * e21 / (1.0 + e21)

    oh1 = (erow == i1).astype(F32)
    oh2 = (erow == i2).astype(F32)
    both = oh1 + oh2
    before = jnp.dot(both.astype(BF16), tri_ref[...], preferred_element_type=F32) + base_ref[...]
    rank1 = jnp.sum(oh1 * before, axis=0, keepdims=True)
    rank2 = jnp.sum(oh2 * before, axis=0, keepdims=True)
    base_ref[...] = base_ref[...] + jnp.sum(both, axis=1, keepdims=True)
    cnt_ref[...] = jnp.broadcast_to(base_ref[...], cnt_ref.shape)

    zi = jnp.zeros((4, i1.shape[1]), jnp.int32)
    ri_ref[0, 0:1, :] = i1
    ri_ref[0, 1:2, :] = i2
    ri_ref[0, 2:3, :] = rank1.astype(jnp.int32)
    ri_ref[0, 3:4, :] = rank2.astype(jnp.int32)
    ri_ref[0, 4:8, :] = zi
    rg_ref[0, 0:1, :] = gate1
    rg_ref[0, 1:2, :] = gate2
    rg_ref[0, 2:8, :] = jnp.zeros((6, i1.shape[1]), F32)


def merge_and_route(xa, modsel, gl, o_att, y_s5, s5u, y_rw, rws, p, tt, t0):
    b, t, d = xa.shape
    ntile = t // tt
    nsel = ntile - t0
    tok = lambda w: pl.BlockSpec((1, tt, w), lambda i, j: (i, j + t0, 0))
    sel = lambda w: pl.BlockSpec((1, tt, w), lambda i, j: (i, j, 0))
    tok2 = lambda w: pl.BlockSpec((2, 1, tt, w), lambda i, j: (0, i, j + t0, 0))
    tmaj = lambda w: pl.BlockSpec((tt, w), lambda i, j: (j + t0, i))
    c2 = lambda s0, s1: pl.BlockSpec((s0, s1), lambda i, j: (0, 0))
    wr = jnp.concatenate([p['router_e_w'].T, p['router_g_w'].T,
                          jnp.zeros((ROUTE_ROWS - N_EXPERTS - N_GROUPS, d), F32)], axis=0)
    br = jnp.concatenate([p['router_e_b'], p['router_g_b'],
                          jnp.zeros((ROUTE_ROWS - N_EXPERTS - N_GROUPS,), F32)]).reshape(ROUTE_ROWS, 1)
    tri = (jnp.arange(tt)[:, None] < jnp.arange(tt)[None, :]).astype(BF16)
    route_spec = pl.BlockSpec((1, 8, tt), lambda i, j: (i * nsel + j, 0, 0))
    return pl.pallas_call(
        _merge_kernel,
        grid=(b, nsel),
        in_specs=[tok(d),
                  pl.BlockSpec((1, 1, 6, d), lambda i, j: (i, jnp.minimum(j + t0, 1), 0, 0)),
                  tok(N_BRANCH * d), sel(ATT_W), tok2(S5_W), tok(S5_W), tmaj(RW_W), tmaj(RW_W),
                  tok(2 * RW_W),
                  c2(1, S5_W), c2(S5_W, 2 * S5_W), c2(1, RW_W), c2(1, RW_W),
                  c2(ATT_W, d), c2(S5_W, d), c2(RW_W, d), c2(d, d),
                  c2(1, d), c2(ROUTE_ROWS, d), c2(ROUTE_ROWS, 1), c2(tt, tt)],
        out_specs=[sel(d), sel(d), route_spec, route_spec, c2(N_EXPERTS, LANES)],
        out_shape=[jax.ShapeDtypeStruct((b, nsel * tt, d), F32),
                   jax.ShapeDtypeStruct((b, nsel * tt, d), F32),
                   jax.ShapeDtypeStruct((b * nsel, 8, tt), jnp.int32),
                   jax.ShapeDtypeStruct((b * nsel, 8, tt), F32),
                   jax.ShapeDtypeStruct((N_EXPERTS, LANES), F32)],
        scratch_shapes=[pltpu.VMEM((N_EXPERTS, 1), F32)],
        compiler_params=_cparams(2),
        name="merge_route",
    )(xa, modsel, gl, o_att, y_s5, s5u, y_rw[0], y_rw[1], rws,
      p['s5_d'].reshape(1, S5_W), p['s5_w_glu'].astype(BF16),
      p['rwkv_ln_w'].reshape(1, RW_W), p['rwkv_ln_b'].reshape(1, RW_W),
      p['proj_att'].astype(BF16), p['proj_s5'].astype(BF16), p['proj_rwkv'].astype(BF16),
      p['w_out'].astype(BF16), p['norm2'].reshape(1, d), wr, br, tri)


DMA_UNROLL = 8


def _row_copies(n_rows, make_copy, sems):
    def start(g, c):
        for u in range(DMA_UNROLL):
            r = g * DMA_UNROLL + u
            for k in range(2):
                make_copy(r, k, sems.at[k]).start(priority=k)
        return c

    def wait(g, c):
        for k in range(2):
            for u in range(DMA_UNROLL):
                make_copy(g * DMA_UNROLL + u, k, sems.at[k]).wait()
        return c

    lax.fori_loop(0, n_rows // DMA_UNROLL, start, 0)
    lax.fori_loop(0, n_rows // DMA_UNROLL, wait, 0)


def _dispatch_kernel(pend_ref, h_ref, dest_ref, hs_ref, zero_ref, sems, zsem):
    tt = h_ref.shape[1]
    n_blocks = hs_ref.shape[0] // MOE_BLOCK
    first = (pl.program_id(0) == 0) & (pl.program_id(1) == 0)

    def zero_copy(start):
        return pltpu.make_async_copy(zero_ref, hs_ref.at[pl.ds(pl.multiple_of(start, MOE_BLOCK), MOE_BLOCK), :],
                                     zsem)

    @pl.when(first)
    def _():
        zero_ref[...] = jnp.zeros_like(zero_ref)
        for e in range(N_EXPERTS):
            @pl.when(pend_ref[e + 1] > pend_ref[e])
            def _():
                zero_copy(pend_ref[e + 1] - MOE_BLOCK).start()
        for e in range(N_EXPERTS):
            @pl.when(pend_ref[e + 1] > pend_ref[e])
            def _():
                zero_copy(pend_ref[e + 1] - MOE_BLOCK).wait()

        def clear(blk, c):
            cp = zero_copy(blk * MOE_BLOCK)
            cp.start()
            cp.wait()
            return c

        lax.fori_loop(pend_ref[N_EXPERTS] // MOE_BLOCK, n_blocks, clear, 0)

    def row_copy(r, k, sem):
        return pltpu.make_async_copy(h_ref.at[0, pl.ds(r, 1), :],
                                     hs_ref.at[pl.ds(dest_ref[0, k, r], 1), :], sem)

    _row_copies(tt, row_copy, sems)


def moe_dispatch(h2, dest, pad_end0, n_slots, tt):
    b, t, d = h2.shape
    nsel = t // tt
    grid_spec = pltpu.PrefetchScalarGridSpec(
        num_scalar_prefetch=1,
        grid=(b, nsel),
        in_specs=[pl.BlockSpec((1, tt, d), lambda i, j, pe: (i, j, 0)),
                  pl.BlockSpec((1, 2, tt), lambda i, j, pe: (i * nsel + j, 0, 0),
                               memory_space=pltpu.SMEM)],
        out_specs=pl.BlockSpec(memory_space=pl.ANY),
        scratch_shapes=[pltpu.VMEM((MOE_BLOCK, d), F32),
                        pltpu.SemaphoreType.DMA((2,)), pltpu.SemaphoreType.DMA(())],
    )
    return pl.pallas_call(
        _dispatch_kernel,
        grid_spec=grid_spec,
        out_shape=jax.ShapeDtypeStruct((n_slots, d), F32),
        compiler_params=_cparams(2),
        name="moe_dispatch",
    )(pad_end0, h2, dest)


def _expert_kernel(be_ref, nu_ref, x_ref, wg_ref, wu_ref, wd_ref, o_ref):
    del be_ref
    used = pl.program_id(0) < nu_ref[0]

    @pl.when(used)
    def _():
        x = x_ref[...].astype(BF16)
        a = jnp.dot(x, wg_ref[0], preferred_element_type=F32)
        u = jnp.dot(x, wu_ref[0], preferred_element_type=F32)
        hmid = (a * _sigmoid(a) * u).astype(BF16)
        o_ref[...] = jnp.dot(hmid, wd_ref[0], preferred_element_type=F32)

    @pl.when(jnp.logical_not(used))
    def _():
        o_ref[...] = jnp.zeros_like(o_ref)


def moe_experts(hs, blk_expert, n_used, w_gate, w_up, w_down):
    n_slots, d = hs.shape
    f = w_gate.shape[2]
    nblk = n_slots // MOE_BLOCK
    rows = lambda i, be, nu: (jnp.minimum(i, nu[0] - 1), 0)
    wsel = lambda i, be, nu: (be[jnp.minimum(i, nu[0] - 1)], 0, 0)
    grid_spec = pltpu.PrefetchScalarGridSpec(
        num_scalar_prefetch=2,
        grid=(nblk,),
        in_specs=[pl.BlockSpec((MOE_BLOCK, d), rows),
                  pl.BlockSpec((1, d, f), wsel), pl.BlockSpec((1, d, f), wsel),
                  pl.BlockSpec((1, f, d), wsel)],
        out_specs=pl.BlockSpec((MOE_BLOCK, d), lambda i, be, nu: (i, 0)),
    )
    return pl.pallas_call(
        _expert_kernel,
        grid_spec=grid_spec,
        out_shape=jax.ShapeDtypeStruct((n_slots, d), F32),
        compiler_params=_cparams(1),
        name="moe_experts",
    )(blk_expert, n_used, hs, w_gate.astype(BF16), w_up.astype(BF16), w_down.astype(BF16))


def _combine_kernel(x_ref, mod_ref, dest_ref, gate_ref, ys_ref, o_ref, buf_ref, sems):
    tt = x_ref.shape[1]

    def row_copy(r, k, sem):
        return pltpu.make_async_copy(ys_ref.at[pl.ds(dest_ref[0, k, r], 1), :],
                                     buf_ref.at[k, pl.ds(r, 1), :], sem)

    _row_copies(tt, row_copy, sems)
    y = gate_ref[0, :, 0:1] * buf_ref[0] + gate_ref[0, :, 1:2] * buf_ref[1]
    o_ref[0] = x_ref[0] + mod_ref[0, 0, 5:6, :] * y


def moe_combine(x_mid, modsel, dest, gates_t, ys, tt, t0):
    b, t, d = x_mid.shape
    nsel = t // tt
    grid_spec = pltpu.PrefetchScalarGridSpec(
        num_scalar_prefetch=0,
        grid=(b, nsel),
        in_specs=[pl.BlockSpec((1, tt, d), lambda i, j: (i, j, 0)),
                  pl.BlockSpec((1, 1, 6, d), lambda i, j: (i, jnp.minimum(j + t0, 1), 0, 0)),
                  pl.BlockSpec((1, 2, tt), lambda i, j: (i * nsel + j, 0, 0), memory_space=pltpu.SMEM),
                  pl.BlockSpec((1, tt, 2), lambda i, j: (i * nsel + j, 0, 0)),
                  pl.BlockSpec(memory_space=pl.ANY)],
        out_specs=pl.BlockSpec((1, tt, d), lambda i, j: (i, j, 0)),
        scratch_shapes=[pltpu.VMEM((2, tt, d), F32), pltpu.SemaphoreType.DMA((2,))],
    )
    return pl.pallas_call(
        _combine_kernel,
        grid_spec=grid_spec,
        out_shape=jax.ShapeDtypeStruct((b, t, d), F32),
        compiler_params=_cparams(2),
        name="moe_combine",
    )(x_mid, modsel, dest, gates_t, ys)


def moe_layer(x_mid, h2, ri, rg, cnt, modsel, p, tt, t0):
    b, t, d = x_mid.shape
    nsel = t // tt
    n_tok = b * nsel * tt
    n_blocks = -(-(n_tok * 2) // MOE_BLOCK) + N_EXPERTS
    n_slots = n_blocks * MOE_BLOCK
    counts = cnt[:, 0].astype(jnp.int32)
    padded = (counts + MOE_BLOCK - 1) // MOE_BLOCK * MOE_BLOCK
    pad_end = jnp.cumsum(padded)
    pad_start = pad_end - padded
    dest = pad_start[ri[:, 0:2, :]] + ri[:, 2:4, :]
    blk_start = jnp.arange(n_blocks, dtype=jnp.int32) * MOE_BLOCK
    blk_expert = jnp.minimum(jnp.sum(blk_start[:, None] >= pad_end[None, :], axis=1),
                             N_EXPERTS - 1).astype(jnp.int32)
    n_used = (pad_end[-1:] // MOE_BLOCK).astype(jnp.int32)
    pad_end0 = jnp.concatenate([jnp.zeros((1,), jnp.int32), pad_end.astype(jnp.int32)])
    hs = moe_dispatch(h2, dest, pad_end0, n_slots, tt)
    ys = moe_experts(hs, blk_expert, n_used, p['exp_w_gate'], p['exp_w_up'], p['exp_w_down'])
    gates_t = jnp.transpose(rg[:, 0:2, :], (0, 2, 1))
    return moe_combine(x_mid, modsel, dest, gates_t, ys, tt, t0)


def kernel(x, c, ctx, c_ctx, w_mod, b_mod, norm1, w_in, q_gain, k_gain, s5_a_re, s5_a_im, s5_log_dt, s5_b_re, s5_b_im, s5_c_re, s5_c_im, s5_d, s5_w_glu, rwkv_conv, rwkv_w0, rwkv_w2, rwkv_a0, rwkv_a2, rwkv_g2, rwkv_k_k, rwkv_k_a, rwkv_r_k, rwkv_ln_w, rwkv_ln_b, proj_att, proj_s5, proj_rwkv, w_out, norm2, router_g_w, router_g_b, router_e_w, router_e_b, exp_w_gate, exp_w_up, exp_w_down):
    b, l_len, d = x.shape
    c_len = ctx.shape[1]
    depth = w_mod.shape[0]
    t = c_len + l_len
    tt = 256 if c_len % 256 == 0 else c_len
    assert l_len % tt == 0 and tt % S5_CHUNK == 0 and tt % RW_TB == 0 and tt % DMA_UNROLL == 0
    assert b * RW_H == LANES, "the recurrence puts (batch, head) on the 128 lanes"

    rows = -(-(b + 1) // 8) * 8
    cc = jnp.concatenate([c, c_ctx[None], jnp.zeros((rows - b - 1, d), F32)], axis=0)
    mods = modulation(cc, w_mod, b_mod)
    cos_t, sin_t = rope_tables(c_len, l_len)

    xa = jnp.concatenate([ctx, x], axis=1)
    for l in range(depth):
        need_ctx = l < depth - 1
        t0 = 0 if need_ctx else c_len // tt
        mod_lat = mods[l, :b].reshape(b, 1, 6, d)
        mod_ctx = jnp.broadcast_to(mods[l, b].reshape(1, 1, 6, d), (b, 1, 6, d))
        modsel = jnp.concatenate([mod_ctx, mod_lat], axis=1)

        a, s5u, z, gl = in_projection(xa, modsel, norm1[l], w_in[l], tt, 0)
        o_att = attention(a, cos_t, sin_t, q_gain[l], k_gain[l], tt, t0, c_len)
        ops = s5_operators(s5_a_re[l], s5_a_im[l], s5_log_dt[l], s5_b_re[l], s5_b_im[l],
                           s5_c_re[l], s5_c_im[l])
        y_s5 = s5_scan(s5u, ops, c_len)
        rws, streams = rwkv_prepare(z, rwkv_conv[l], rwkv_w0[l], rwkv_w2[l], rwkv_a0[l], rwkv_a2[l],
                                    rwkv_g2[l], rwkv_k_k[l], rwkv_k_a[l], rwkv_r_k[l], tt, c_len)
        y_rw = rwkv_scan(streams, b, c_len)
        p = dict(s5_d=s5_d[l], s5_w_glu=s5_w_glu[l], rwkv_ln_w=rwkv_ln_w[l], rwkv_ln_b=rwkv_ln_b[l],
                 proj_att=proj_att[l], proj_s5=proj_s5[l], proj_rwkv=proj_rwkv[l], w_out=w_out[l],
                 norm2=norm2[l], router_g_w=router_g_w[l], router_g_b=router_g_b[l],
                 router_e_w=router_e_w[l], router_e_b=router_e_b[l],
                 exp_w_gate=exp_w_gate[l], exp_w_up=exp_w_up[l], exp_w_down=exp_w_down[l])
        x_mid, h2, ri, rg, cnt = merge_and_route(xa, modsel, gl, o_att, y_s5, s5u, y_rw, rws, p, tt, t0)
        xa = moe_layer(x_mid, h2, ri, rg, cnt, modsel, p, tt, t0)
    return xa
```

```python
import functools

import jax
import jax.numpy as jnp
from jax import lax
from jax.experimental import pallas as pl
from jax.experimental.pallas import tpu as pltpu

F32 = jnp.float32
BF16 = jnp.bfloat16
HIGHEST = lax.Precision.HIGHEST

EPS = 1e-6
GRID_W = 64
ROPE_THETA = 10000.0

HEAD_DIM = 64
ATT_W = 512
KV_W = 256
ATT_IN = ATT_W + 2 * KV_W

S5_W = 256
S5_CG = 16
S5_G = 16
S5_N = 64
S5_CHUNK = 16

RW_W = 256
RW_N = 64
RW_H = 4
LORA = 64
GATE_LORA = 128
RW_STREAM = 3 * RW_W + 2 * LORA + GATE_LORA
GN_EPS = 64e-5
RW_TB = 16
RW_IQ = 16

N_BRANCH = 3
N_GROUPS = 4
EPG = 8
N_EXPERTS = 32
MOE_BLOCK = 256
ROUTE_ROWS = 40

LANES = 128
VMEM_LIMIT = 56 * 1024 * 1024


def _cparams(n_axes, vmem=VMEM_LIMIT):
    return pltpu.CompilerParams(dimension_semantics=("arbitrary",) * n_axes,
                                vmem_limit_bytes=vmem)


def _sigmoid(x):
    return 1.0 / (1.0 + jnp.exp(-x))


def _seg64_sum(x):
    parts = []
    for s in range(x.shape[-1] // LANES):
        xs = x[:, s * LANES:(s + 1) * LANES]
        lo = lax.broadcasted_iota(jnp.int32, xs.shape, 1) < 64
        s_lo = jnp.sum(jnp.where(lo, xs, 0.0), axis=-1, keepdims=True)
        s_hi = jnp.sum(jnp.where(lo, 0.0, xs), axis=-1, keepdims=True)
        parts.append(jnp.where(lo, s_lo, s_hi))
    return parts[0] if len(parts) == 1 else jnp.concatenate(parts, axis=-1)


def _mod_kernel(c_ref, w_ref, b_ref, o_ref):
    c = c_ref[...]
    s = c * _sigmoid(c)
    o_ref[0] = jnp.dot(s, w_ref[0], precision=HIGHEST, preferred_element_type=F32) + b_ref[0]


def modulation(cc, w_mod, b_mod):
    depth, d, n = w_mod.shape
    rows = cc.shape[0]
    nt = n // 4
    return pl.pallas_call(
        _mod_kernel,
        grid=(depth, n // nt),
        in_specs=[pl.BlockSpec((rows, d), lambda l, j: (0, 0)),
                  pl.BlockSpec((1, d, nt), lambda l, j: (l, 0, j)),
                  pl.BlockSpec((1, 1, nt), lambda l, j: (l, 0, j))],
        out_specs=pl.BlockSpec((1, rows, nt), lambda l, j: (l, 0, j)),
        out_shape=jax.ShapeDtypeStruct((depth, rows, n), F32),
        compiler_params=_cparams(2),
        name="modulation",
    )(cc, w_mod, b_mod.reshape(depth, 1, n))


def _inproj_kernel(x_ref, mod_ref, n1_ref, wa_ref, ws_ref, wr_ref, wg_ref,
                   a_ref, s_ref, z_ref, g_ref):
    x = x_ref[0]
    ms = jnp.mean(x * x, axis=-1, keepdims=True)
    y = x * lax.rsqrt(ms + EPS) * n1_ref[...]
    h = (y * (1.0 + mod_ref[0, 0, 1:2, :]) + mod_ref[0, 0, 0:1, :]).astype(BF16)
    a_ref[0] = jnp.dot(h, wa_ref[...], preferred_element_type=F32).astype(BF16)
    s_ref[0] = jnp.dot(h, ws_ref[...], preferred_element_type=F32)
    z_ref[0] = jnp.dot(h, wr_ref[...], preferred_element_type=F32)
    g_ref[0] = jnp.dot(h, wg_ref[...], preferred_element_type=F32).astype(BF16)


def in_projection(xa, modsel, norm1, w_in, tt, t0):
    b, t, d = xa.shape
    ntile = t // tt
    off_s5 = ATT_IN
    off_rw = off_s5 + S5_W
    off_gate = off_rw + RW_STREAM
    wa = w_in[:, :off_s5].astype(BF16)
    ws = w_in[:, off_s5:off_rw].astype(BF16)
    wr = w_in[:, off_rw:off_gate].astype(BF16)
    wg = w_in[:, off_gate:].astype(BF16)
    tok = lambda w: pl.BlockSpec((1, tt, w), lambda i, j: (i, j + t0, 0))
    full = lambda w: pl.BlockSpec((d, w), lambda i, j: (0, 0))
    return pl.pallas_call(
        _inproj_kernel,
        grid=(b, ntile - t0),
        in_specs=[tok(d),
                  pl.BlockSpec((1, 1, 6, d), lambda i, j: (i, jnp.minimum(j + t0, 1), 0, 0)),
                  pl.BlockSpec((1, d), lambda i, j: (0, 0)),
                  full(ATT_IN), full(S5_W), full(RW_STREAM), full(N_BRANCH * d)],
        out_specs=[tok(ATT_IN), tok(S5_W), tok(RW_STREAM), tok(N_BRANCH * d)],
        out_shape=[jax.ShapeDtypeStruct((b, t, ATT_IN), BF16),
                   jax.ShapeDtypeStruct((b, t, S5_W), F32),
                   jax.ShapeDtypeStruct((b, t, RW_STREAM), F32),
                   jax.ShapeDtypeStruct((b, t, N_BRANCH * d), BF16)],
        compiler_params=_cparams(2),
        name="in_projection",
    )(xa, modsel, norm1.reshape(1, d), wa, ws, wr, wg)


def _headnorm(x, gain):
    ms = _seg64_sum(x * x) * (1.0 / HEAD_DIM)
    return x * lax.rsqrt(ms + EPS) * gain


def _rope(x, cos, sin):
    lane = lax.broadcasted_iota(jnp.int32, x.shape, 1)
    first = (lane & 16) == 0
    partner = jnp.where(first, pltpu.roll(x, LANES - 16, 1), pltpu.roll(x, 16, 1))
    return x * cos + partner * sin


def _attend_slab(q, hh, kn, vmt):
    lane = lax.broadcasted_iota(jnp.int32, q.shape, 1)
    kv_lanes = (lane >= 64) if hh == 1 else (lane < 64)
    q_sw = pltpu.roll(q, 64, 1)
    srcs = (q, q_sw) if hh == 0 else (q_sw, q)
    rows = q.shape[0]
    qm = jnp.concatenate([jnp.where(kv_lanes, src, 0.0) for src in srcs], axis=0).astype(BF16)
    st = lax.dot_general(kn, qm, (((1,), (1,)), ((), ())), preferred_element_type=F32)
    m = jnp.max(st, axis=0, keepdims=True)
    et = jnp.exp((st - m).astype(BF16))
    pvt = jnp.dot(vmt, et, preferred_element_type=F32)
    pvt = pvt / pltpu.roll(pvt, 64, 0)
    own = slice(64 * hh, 64 * hh + 64)
    out_t = jnp.concatenate([pvt[own, :rows], pvt[own, rows:]], axis=0)
    return out_t.T


def _attn_kernel(q_ref, kv_ref, cq_ref, sq_ref, ck_ref, sk_ref, qg_ref, kg_ref,
                 o_ref, kn_ref, vm_ref, *, t0, c_len):
    j = pl.program_id(1)
    t_all = kv_ref.shape[1]

    @pl.when(j == 0)
    def _():
        for p in range(2):
            k = kv_ref[0, :, p * LANES:(p + 1) * LANES].astype(F32)
            k = _rope(_headnorm(k, kg_ref[...]), ck_ref[...], sk_ref[...])
            kn_ref[p] = k.astype(BF16)
            vt = kv_ref[0, :, KV_W + p * LANES:KV_W + (p + 1) * LANES].astype(F32).T
            low = lax.broadcasted_iota(jnp.int32, vt.shape, 0) < 64
            vm_ref[2 * p] = jnp.where(low, vt, 1.0).astype(BF16)
            vm_ref[2 * p + 1] = jnp.where(low, 1.0, vt).astype(BF16)

    def run(n_keys):
        for p in range(2):
            kn = kn_ref[p, 0:n_keys, :]
            for hh in range(2):
                slab = 2 * p + hh
                vp = vm_ref[slab, :, 0:n_keys]
                q = q_ref[0, :, slab * LANES:(slab + 1) * LANES].astype(F32)
                q = _rope(_headnorm(q, qg_ref[...]), cq_ref[...], sq_ref[...]) * (HEAD_DIM ** -0.5)
                o_ref[0, :, slab * LANES:(slab + 1) * LANES] = _attend_slab(q, hh, kn, vp).astype(BF16)

    if t0 == 0:
        pl.when(j == 0)(lambda: run(c_len))
        pl.when(j > 0)(lambda: run(t_all))
    else:
        run(t_all)


def attention(a, cos_t, sin_t, q_gain, k_gain, tt, t0, c_len):
    b, t, _ = a.shape
    ntile = t // tt
    qg = jnp.tile(q_gain, 2).reshape(1, LANES)
    kg = jnp.tile(k_gain, 2).reshape(1, LANES)
    tab_q = pl.BlockSpec((tt, LANES), lambda i, j: (j + t0, 0))
    tab_k = pl.BlockSpec((t, LANES), lambda i, j: (0, 0))
    vec = pl.BlockSpec((1, LANES), lambda i, j: (0, 0))
    return pl.pallas_call(
        functools.partial(_attn_kernel, t0=t0, c_len=c_len),
        grid=(b, ntile - t0),
        in_specs=[pl.BlockSpec((1, tt, ATT_W), lambda i, j: (i, j + t0, 0)),
                  pl.BlockSpec((1, t, ATT_W), lambda i, j: (i, 0, 1)),
                  tab_q, tab_q, tab_k, tab_k, vec, vec],
        out_specs=pl.BlockSpec((1, tt, ATT_W), lambda i, j: (i, j, 0)),
        out_shape=jax.ShapeDtypeStruct((b, t - t0 * tt, ATT_W), BF16),
        scratch_shapes=[pltpu.VMEM((2, t, LANES), BF16), pltpu.VMEM((4, LANES, t), BF16)],
        compiler_params=_cparams(2),
        name="attention",
    )(a, a, cos_t, sin_t, cos_t, sin_t, qg, kg)


def rope_tables(c_len, l_len):
    half = HEAD_DIM // 4
    n = jnp.arange(l_len, dtype=jnp.int32)
    inv = ROPE_THETA ** (-jnp.arange(half, dtype=F32) / half)
    ar = (n // GRID_W).astype(F32)[:, None] * inv[None, :]
    ac = (n % GRID_W).astype(F32)[:, None] * inv[None, :]
    cos = jnp.concatenate([jnp.cos(ar), jnp.cos(ar), jnp.cos(ac), jnp.cos(ac)], axis=-1)
    sin = jnp.concatenate([-jnp.sin(ar), jnp.sin(ar), -jnp.sin(ac), jnp.sin(ac)], axis=-1)
    cos = jnp.concatenate([jnp.ones((c_len, HEAD_DIM), F32), cos], axis=0)
    sin = jnp.concatenate([jnp.zeros((c_len, HEAD_DIM), F32), sin], axis=0)
    return jnp.tile(cos, (1, 2)), jnp.tile(sin, (1, 2))


def _s5_kernel(x_ref, mt_ref, qt_ref, pt_ref, ar_ref, ai_ref, y_ref, hloc_ref, hs_ref, *, nb, n_ctx):
    d = pl.program_id(0)
    x = x_ref[0]
    hloc_ref[...] = jnp.dot(x, qt_ref[0, 0], preferred_element_type=F32)
    n_chunks = x.shape[0] // nb
    a_r = ar_ref[0, 0]
    a_i = ai_ref[0, 0]

    def body(k, h):
        k_rev = jnp.where(k < n_ctx, n_ctx - 1 - k, n_chunks - 1 - (k - n_ctx))
        c = jnp.where(d == 0, k, k_rev)
        rows = pl.ds(pl.multiple_of(c * nb, nb), nb)
        hs_ref[rows, :] = h
        return h * a_r + pltpu.roll(h, S5_N, 1) * a_i + hloc_ref[rows, :]

    lax.fori_loop(0, n_chunks, body, jnp.zeros((nb, 2 * S5_N), F32))
    y_ref[0, 0] = (jnp.dot(x, mt_ref[0, 0], preferred_element_type=F32)
                   + jnp.dot(hs_ref[...].astype(BF16), pt_ref[0, 0], preferred_element_type=F32))


def s5_operators(a_re, a_im, log_dt, b_re, b_im, c_re, c_im):
    tc = S5_CHUNK
    dt = jnp.exp(log_dt.astype(F32))[..., None]
    lam_re = a_re.astype(F32)
    lam_im = a_im.astype(F32)
    mag = jnp.exp(lam_re * dt)
    ab_re = mag * jnp.cos(lam_im * dt)
    ab_im = mag * jnp.sin(lam_im * dt)
    den = lam_re * lam_re + lam_im * lam_im
    nr = ab_re - 1.0
    z_re = (nr * lam_re + ab_im * lam_im) / den
    z_im = (ab_im * lam_re - nr * lam_im) / den
    br = b_re.astype(F32)
    bi = b_im.astype(F32)
    bb_re = z_re[..., None] * br - z_im[..., None] * bi
    bb_im = z_re[..., None] * bi + z_im[..., None] * br
    k = jnp.arange(tc + 1, dtype=F32)[:, None, None, None]
    pw_mag = jnp.exp(k * (lam_re * dt)[None])
    pw_re = pw_mag * jnp.cos(k * (lam_im * dt)[None])
    pw_im = pw_mag * jnp.sin(k * (lam_im * dt)[None])
    cr = c_re.astype(F32)
    ci = c_im.astype(F32)
    ca_re = cr[None] * pw_re[:, :, :, None, :] - ci[None] * pw_im[:, :, :, None, :]
    ca_im = cr[None] * pw_im[:, :, :, None, :] + ci[None] * pw_re[:, :, :, None, :]
    kern = (jnp.einsum('kdgcn,dgne->kdgce', ca_re[:tc], bb_re, precision=HIGHEST)
            - jnp.einsum('kdgcn,dgne->kdgce', ca_im[:tc], bb_im, precision=HIGHEST))
    s_idx = jnp.arange(tc)[:, None]
    t_idx = jnp.arange(tc)[None, :]
    mts, pts, qts = [], [], []
    for d in range(2):
        lag = (t_idx - s_idx) if d == 0 else (s_idx - t_idx)
        toe = kern[:, d][jnp.clip(lag, 0, tc - 1)]
        toe = jnp.where((lag >= 0)[:, :, None, None, None], toe, 0.0)
        mts.append(jnp.transpose(toe, (2, 0, 4, 1, 3)).reshape(S5_G, tc * S5_CG, tc * S5_CG))
        steps = slice(1, tc + 1) if d == 0 else slice(tc, 0, -1)
        p_re = jnp.transpose(ca_re[steps, d], (1, 3, 0, 2))
        p_im = jnp.transpose(ca_im[steps, d], (1, 3, 0, 2))
        pts.append(jnp.concatenate([p_re, -p_im], axis=1).reshape(S5_G, 2 * S5_N, tc * S5_CG))
        order = slice(tc - 1, None, -1) if d == 0 else slice(0, tc)
        rev_re = pw_re[order, d]
        rev_im = pw_im[order, d]
        q_re = rev_re[..., None] * bb_re[d][None] - rev_im[..., None] * bb_im[d][None]
        q_im = rev_re[..., None] * bb_im[d][None] + rev_im[..., None] * bb_re[d][None]
        q_re = jnp.transpose(q_re, (1, 0, 3, 2))
        q_im = jnp.transpose(q_im, (1, 0, 3, 2))
        qts.append(jnp.concatenate([q_re, q_im], axis=-1).reshape(S5_G, tc * S5_CG, 2 * S5_N))
    a_row = jnp.concatenate([pw_re[tc], pw_re[tc]], axis=-1)[:, :, None, :]
    ai_row = jnp.concatenate([-pw_im[tc], pw_im[tc]], axis=-1)[:, :, None, :]
    return (jnp.stack(mts).astype(BF16), jnp.stack(qts).astype(BF16), jnp.stack(pts).astype(BF16),
            a_row, ai_row)


def s5_scan(s5u, ops, c_len):
    b, t, _ = s5u.shape
    tc = S5_CHUNK
    nch = t // tc
    mt, qt, pt, a_row, ai_row = ops
    x = s5u.reshape(b, nch, tc, S5_G, S5_CG)
    x = jnp.transpose(x, (3, 1, 0, 2, 4)).reshape(S5_G, nch * b, tc * S5_CG).astype(BF16)
    r = nch * b
    w = tc * S5_CG
    spec = lambda s0, s1: pl.BlockSpec((1, 1, s0, s1), lambda d, g: (d, g, 0, 0))
    y = pl.pallas_call(
        functools.partial(_s5_kernel, nb=b, n_ctx=c_len // tc),
        grid=(2, S5_G),
        in_specs=[pl.BlockSpec((1, r, w), lambda d, g: (g, 0, 0)),
                  spec(w, w), spec(w, 2 * S5_N), spec(2 * S5_N, w),
                  spec(1, 2 * S5_N), spec(1, 2 * S5_N)],
        out_specs=spec(r, w),
        out_shape=jax.ShapeDtypeStruct((2, S5_G, r, w), F32),
        scratch_shapes=[pltpu.VMEM((r, 2 * S5_N), F32), pltpu.VMEM((r, 2 * S5_N), F32)],
        compiler_params=_cparams(2),
        name="s5_scan",
    )(x, mt, qt, pt, a_row, ai_row)
    y = y.reshape(2, S5_G, nch, b, tc, S5_CG)
    return jnp.transpose(y, (0, 3, 2, 4, 1, 5)).reshape(2, b, t, S5_W)


RW_STREAMS = 9


def _rwprep_kernel(z_ref, conv_ref, w0_ref, w2_ref, a0_ref, a2_ref, g2_ref, kk_ref, ka_ref, rk_ref,
                   fin_ref, *stream_refs, tt, c_len):
    t_all = z_ref.shape[1]
    r0 = pl.multiple_of(pl.program_id(1) * tt, tt)
    zc = z_ref[0, pl.ds(r0, tt), :]
    row = lax.broadcasted_iota(jnp.int32, zc.shape, 0)
    has_prev = (r0 != 0) & (r0 != c_len)
    has_next = (r0 + tt != c_len) & (r0 + tt != t_all)
    prev_row = z_ref[0, pl.ds(jnp.maximum(r0 - 1, 0), 1), :]
    next_row = z_ref[0, pl.ds(jnp.minimum(r0 + tt, t_all - 1), 1), :]
    prev_row = jnp.where(has_prev, prev_row, 0.0)
    next_row = jnp.where(has_next, next_row, 0.0)
    zp = jnp.where(row == 0, prev_row, pltpu.roll(zc, 1, 0))
    zn = jnp.where(row == tt - 1, next_row, pltpu.roll(zc, tt - 1, 0))
    zz = zp * conv_ref[0:1, :] + zc * conv_ref[1:2, :] + zn * conv_ref[2:3, :]
    r = zz[:, 0:RW_W]
    k = zz[:, RW_W:2 * RW_W]
    v = zz[:, 2 * RW_W:3 * RW_W]
    lora = zz[:, 3 * RW_W:3 * RW_W + 2 * LORA]
    xg = zz[:, 3 * RW_W + 2 * LORA:]
    kk = k * kk_ref[...]
    kk = kk * lax.rsqrt(_seg64_sum(kk * kk) + 1e-12)
    lora_w = jnp.tanh(lora).astype(BF16)
    lora_a = lora.astype(BF16)
    outs = [r, v, kk]
    kt_sum = None
    for d in range(2):
        wl = w0_ref[d:d + 1, :] + jnp.dot(lora_w, w2_ref[d], preferred_element_type=F32)
        sp = jnp.maximum(-wl, 0.0) + jnp.log(1.0 + jnp.exp(-jnp.abs(wl)))
        decay = jnp.exp(-jnp.exp(-sp - 0.5))
        a = _sigmoid(a0_ref[d:d + 1, :] + jnp.dot(lora_a, a2_ref[d], preferred_element_type=F32))
        kt = k * (1.0 + (a - 1.0) * ka_ref[...])
        outs += [decay, kk * a, kt]
        kt_sum = kt if kt_sum is None else kt_sum + kt
    bonus = _seg64_sum(r * kt_sum * rk_ref[...])
    gate = jnp.dot(_sigmoid(xg).astype(BF16), g2_ref[...], preferred_element_type=F32)
    fin_ref[0, :, 0:RW_W] = bonus * v
    fin_ref[0, :, RW_W:2 * RW_W] = gate
    for ref, val in zip(stream_refs, outs):
        ref[...] = val


def rwkv_prepare(z, conv_w, w0, w2, a0, a2, g2, k_k, k_a, r_k, tt, c_len):
    b, t, _ = z.shape
    zeros = jnp.zeros((2, LORA, RW_W), F32)
    w2p = jnp.concatenate([w2, zeros], axis=1).astype(BF16)
    a2p = jnp.concatenate([zeros, a2], axis=1).astype(BF16)
    row = lambda v: v.reshape(1, RW_W)
    c2 = lambda s0, s1: pl.BlockSpec((s0, s1), lambda i, j: (0, 0))
    c3 = lambda s0, s1, s2: pl.BlockSpec((s0, s1, s2), lambda i, j: (0, 0, 0))
    outs = pl.pallas_call(
        functools.partial(_rwprep_kernel, tt=tt, c_len=c_len),
        grid=(b, t // tt),
        in_specs=[pl.BlockSpec((1, t, RW_STREAM), lambda i, j: (i, 0, 0)),
                  c2(3, RW_STREAM), c2(2, RW_W), c3(2, 2 * LORA, RW_W), c2(2, RW_W),
                  c3(2, 2 * LORA, RW_W), c2(GATE_LORA, RW_W), c2(1, RW_W), c2(1, RW_W), c2(1, RW_W)],
        out_specs=[pl.BlockSpec((1, tt, 2 * RW_W), lambda i, j: (i, j, 0))]
        + [pl.BlockSpec((tt, RW_W), lambda i, j: (j, i))] * RW_STREAMS,
        out_shape=[jax.ShapeDtypeStruct((b, t, 2 * RW_W), F32)]
        + [jax.ShapeDtypeStruct((t, b * RW_W), F32)] * RW_STREAMS,
        compiler_params=_cparams(2),
        name="rwkv_prepare",
    )(z, conv_w, w0, w2p, a0, a2p, g2.astype(BF16), row(k_k), row(k_a), row(r_k))
    return outs[0], outs[1:]


RW_DIR_INPUTS = 7


def _key_row(ref, lead, j):
    return ref[(*lead, pl.ds(j, RW_IQ, stride=0), slice(None))]


def _rwscan_kernel(*refs):
    ins = refs[:2 * RW_DIR_INPUTS]
    y_refs = refs[2 * RW_DIR_INPUTS:2 * RW_DIR_INPUTS + 2]
    s_ref, sa_ref, kkx_ref, wc_ref, rows_ref = refs[2 * RW_DIR_INPUTS + 2:]
    tb = y_refs[0].shape[0]

    @pl.when(pl.program_id(0) == 0)
    def _():
        s_ref[...] = jnp.zeros_like(s_ref)
        sa_ref[...] = jnp.zeros_like(sa_ref)
        wc_ref[...] = jnp.ones_like(wc_ref)

    kkx_ref[0, 0:tb] = ins[5][...]
    kkx_ref[0, tb:tb + 1] = ins[6][0:1]
    kkx_ref[1, 1:tb + 1] = ins[RW_DIR_INPUTS + 5][...]
    kkx_ref[1, 0:1] = ins[RW_DIR_INPUTS + 6][tb - 1:tb]

    def step(tt, block_start):
        for d in range(2):
            w_ref, kka_ref, kt_ref, r_ref, v_ref = ins[RW_DIR_INPUTS * d:RW_DIR_INPUTS * d + 5]
            row = tt if d == 0 else tb - 1 - tt
            nxt = tt + 1 if d == 0 else tb - 1 - tt
            if block_start:
                rows_ref[d, 4] = wc_ref[d]
                wc = w_ref[row]
            else:
                wc = wc_ref[d] * w_ref[row]
            wc_ref[d] = wc
            inv = 1.0 / wc
            rows_ref[d, 0] = kka_ref[row] * inv
            rows_ref[d, 1] = kt_ref[row] * inv
            rows_ref[d, 2] = kkx_ref[d, nxt] * wc
            rows_ref[d, 3] = r_ref[row] * wc

            for q in range(RW_N // RW_IQ):
                rows = slice(q * RW_IQ, (q + 1) * RW_IQ)
                sa = sa_ref[d, rows, :]
                v = v_ref[row, rows, :]
                acc_sa = jnp.zeros_like(sa)
                acc_y = jnp.zeros_like(sa)
                for j in range(RW_N):
                    s_old = s_ref[j, d, rows, :]
                    if block_start:
                        s_old = s_old * _key_row(rows_ref, (d, 4), j)
                    s_new = s_old - sa * _key_row(rows_ref, (d, 0), j) + v * _key_row(rows_ref, (d, 1), j)
                    s_ref[j, d, rows, :] = s_new
                    acc_sa = acc_sa + s_new * _key_row(rows_ref, (d, 2), j)
                    acc_y = acc_y + s_new * _key_row(rows_ref, (d, 3), j)
                sa_ref[d, rows, :] = acc_sa
                y_refs[d][row, rows, :] = acc_y

    step(0, True)

    def body(tt, carry):
        step(tt, False)
        return carry

    lax.fori_loop(1, tb, body, 0)


def rwkv_scan(streams, b, c_len):
    t = streams[0].shape[0]
    ni = b * RW_H
    n_blk = t // RW_TB
    n_ctx = c_len // RW_TB
    r, v, kk, w0, kka0, kt0, w1, kka1, kt1 = [
        jnp.swapaxes(s.reshape(t, ni, RW_N), 1, 2) for s in streams]

    fwd = lambda i: i
    rev = lambda i: jnp.where(i < n_ctx, n_ctx - 1 - i, n_blk - 1 - (i - n_ctx))
    after = lambda i: jnp.minimum(i + 1, n_blk - 1)
    spec = lambda f: pl.BlockSpec((RW_TB, RW_N, ni), lambda i: (f(i), 0, 0))
    dir_specs = lambda f: [spec(f)] * 6 + [spec(lambda i: f(after(i)))]
    y0, y1 = pl.pallas_call(
        _rwscan_kernel,
        grid=(n_blk,),
        in_specs=dir_specs(fwd) + dir_specs(rev),
        out_specs=[spec(fwd), spec(rev)],
        out_shape=[jax.ShapeDtypeStruct((t, RW_N, ni), F32)] * 2,
        scratch_shapes=[pltpu.VMEM((RW_N, 2, RW_N, ni), F32), pltpu.VMEM((2, RW_N, ni), F32),
                        pltpu.VMEM((2, RW_TB + 1, RW_N, ni), F32), pltpu.VMEM((2, RW_N, ni), F32),
                        pltpu.VMEM((2, 5, RW_N, ni), F32)],
        compiler_params=_cparams(1),
        name="rwkv_scan",
    )(w0, kka0, kt0, r, v, kk, kk, w1, kka1, kt1, r, v, kk, kk)
    back = lambda y: jnp.swapaxes(y, 1, 2).reshape(t, b * RW_W)
    return back(y0), back(y1)


def _merge_kernel(x_ref, mod_ref, gl_ref, oatt_ref, ys5_ref, s5u_ref, yrw0_ref, yrw1_ref, rws_ref,
                  dskip_ref, wglu_ref, lnw_ref, lnb_ref, patt_ref, ps5_ref, prw_ref, wout_ref,
                  n2_ref, wr_ref, br_ref, tri_ref,
                  xo_ref, h2_ref, ri_ref, rg_ref, cnt_ref, base_ref):
    d = x_ref.shape[2]
    first = (pl.program_id(0) == 0) & (pl.program_id(1) == 0)

    @pl.when(first)
    def _():
        base_ref[...] = jnp.zeros_like(base_ref)

    y = ys5_ref[0, 0] + ys5_ref[1, 0] + dskip_ref[...] * s5u_ref[0]
    zg = jnp.dot(jax.nn.gelu(y).astype(BF16), wglu_ref[...], preferred_element_type=F32)
    o_s5 = zg[:, :S5_W] * _sigmoid(zg[:, S5_W:])

    yr = yrw0_ref[...] + yrw1_ref[...]
    mu = _seg64_sum(yr) * (1.0 / RW_N)
    yc = yr - mu
    var = _seg64_sum(yc * yc) * (1.0 / RW_N)
    yn = yc * lax.rsqrt(var + GN_EPS) * lnw_ref[...] + lnb_ref[...]
    o_rw = (yn + rws_ref[0, :, 0:RW_W]) * rws_ref[0, :, RW_W:2 * RW_W]

    gates = _sigmoid(gl_ref[0].astype(F32))
    merged = (gates[:, 0:d] * jnp.dot(oatt_ref[0], patt_ref[...], preferred_element_type=F32)
              + gates[:, d:2 * d] * jnp.dot(o_s5.astype(BF16), ps5_ref[...], preferred_element_type=F32)
              + gates[:, 2 * d:3 * d] * jnp.dot(o_rw.astype(BF16), prw_ref[...], preferred_element_type=F32))
    upd = jnp.dot(merged.astype(BF16), wout_ref[...], preferred_element_type=F32)
    x_new = x_ref[0] + mod_ref[0, 0, 2:3, :] * upd
    xo_ref[0] = x_new

    ms = jnp.mean(x_new * x_new, axis=-1, keepdims=True)
    h2 = (x_new * lax.rsqrt(ms + EPS) * n2_ref[...]) * (1.0 + mod_ref[0, 0, 4:5, :]) + mod_ref[0, 0, 3:4, :]
    h2_ref[0] = h2

    logits = lax.dot_general(wr_ref[...], h2, (((1,), (1,)), ((), ())),
                             precision=HIGHEST, preferred_element_type=F32) + br_ref[...]
    gl4 = logits[N_EXPERTS:N_EXPERTS + N_GROUPS]
    grow = lax.broadcasted_iota(jnp.int32, gl4.shape, 0)
    gmax = jnp.max(gl4, axis=0, keepdims=True)
    gsel = jnp.min(jnp.where(gl4 == gmax, grow, N_GROUPS), axis=0, keepdims=True)
    g_w = 1.0 / jnp.sum(jnp.exp(gl4 - gmax), axis=0, keepdims=True)
    el = logits[0:N_EXPERTS]
    erow = lax.broadcasted_iota(jnp.int32, el.shape, 0)
    neg = jnp.float32(-1e30)
    cand = jnp.where((erow >> 3) == gsel, el, neg)
    m1 = jnp.max(cand, axis=0, keepdims=True)
    i1 = jnp.min(jnp.where(cand == m1, erow, N_EXPERTS), axis=0, keepdims=True)
    cand2 = jnp.where(erow == i1, neg, cand)
    m2 = jnp.max(cand2, axis=0, keepdims=True)
    i2 = jnp.min(jnp.where(cand2 == m2, erow, N_EXPERTS), axis=0, keepdims=True)
    e21 = jnp.exp(m2 - m1)
    gate1 = g_w / (1.0 + e21)
    gate2 = g_w * e21 / (1.0 + e21)

    oh1 = (erow == i1).astype(F32)
    oh2 = (erow == i2).astype(F32)
    both = oh1 + oh2
    before = jnp.dot(both.astype(BF16), tri_ref[...], preferred_element_type=F32) + base_ref[...]
    rank1 = jnp.sum(oh1 * before, axis=0, keepdims=True)
    rank2 = jnp.sum(oh2 * before, axis=0, keepdims=True)
    base_ref[...] = base_ref[...] + jnp.sum(both, axis=1, keepdims=True)
    cnt_ref[...] = jnp.broadcast_to(base_ref[...], cnt_ref.shape)

    zi = jnp.zeros((4, i1.shape[1]), jnp.int32)
    ri_ref[0, 0:1, :] = i1
    ri_ref[0, 1:2, :] = i2
    ri_ref[0, 2:3, :] = rank1.astype(jnp.int32)
    ri_ref[0, 3:4, :] = rank2.astype(jnp.int32)
    ri_ref[0, 4:8, :] = zi
    rg_ref[0, 0:1, :] = gate1
    rg_ref[0, 1:2, :] = gate2
    rg_ref[0, 2:8, :] = jnp.zeros((6, i1.shape[1]), F32)


def merge_and_route(xa, modsel, gl, o_att, y_s5, s5u, y_rw, rws, p, tt, t0):
    b, t, d = xa.shape
    ntile = t // tt
    nsel = ntile - t0
    tok = lambda w: pl.BlockSpec((1, tt, w), lambda i, j: (i, j + t0, 0))
    sel = lambda w: pl.BlockSpec((1, tt, w), lambda i, j: (i, j, 0))
    tok2 = lambda w: pl.BlockSpec((2, 1, tt, w), lambda i, j: (0, i, j + t0, 0))
    tmaj = lambda w: pl.BlockSpec((tt, w), lambda i, j: (j + t0, i))
    c2 = lambda s0, s1: pl.BlockSpec((s0, s1), lambda i, j: (0, 0))
    wr = jnp.concatenate([p['router_e_w'].T, p['router_g_w'].T,
                          jnp.zeros((ROUTE_ROWS - N_EXPERTS - N_GROUPS, d), F32)], axis=0)
    br = jnp.concatenate([p['router_e_b'], p['router_g_b'],
                          jnp.zeros((ROUTE_ROWS - N_EXPERTS - N_GROUPS,), F32)]).reshape(ROUTE_ROWS, 1)
    tri = (jnp.arange(tt)[:, None] < jnp.arange(tt)[None, :]).astype(BF16)
    route_spec = pl.BlockSpec((1, 8, tt), lambda i, j: (i * nsel + j, 0, 0))
    return pl.pallas_call(
        _merge_kernel,
        grid=(b, nsel),
        in_specs=[tok(d),
                  pl.BlockSpec((1, 1, 6, d), lambda i, j: (i, jnp.minimum(j + t0, 1), 0, 0)),
                  tok(N_BRANCH * d), sel(ATT_W), tok2(S5_W), tok(S5_W), tmaj(RW_W), tmaj(RW_W),
                  tok(2 * RW_W),
                  c2(1, S5_W), c2(S5_W, 2 * S5_W), c2(1, RW_W), c2(1, RW_W),
                  c2(ATT_W, d), c2(S5_W, d), c2(RW_W, d), c2(d, d),
                  c2(1, d), c2(ROUTE_ROWS, d), c2(ROUTE_ROWS, 1), c2(tt, tt)],
        out_specs=[sel(d), sel(d), route_spec, route_spec, c2(N_EXPERTS, LANES)],
        out_shape=[jax.ShapeDtypeStruct((b, nsel * tt, d), F32),
                   jax.ShapeDtypeStruct((b, nsel * tt, d), F32),
                   jax.ShapeDtypeStruct((b * nsel, 8, tt), jnp.int32),
                   jax.ShapeDtypeStruct((b * nsel, 8, tt), F32),
                   jax.ShapeDtypeStruct((N_EXPERTS, LANES), F32)],
        scratch_shapes=[pltpu.VMEM((N_EXPERTS, 1), F32)],
        compiler_params=_cparams(2),
        name="merge_route",
    )(xa, modsel, gl, o_att, y_s5, s5u, y_rw[0], y_rw[1], rws,
      p['s5_d'].reshape(1, S5_W), p['s5_w_glu'].astype(BF16),
      p['rwkv_ln_w'].reshape(1, RW_W), p['rwkv_ln_b'].reshape(1, RW_W),
      p['proj_att'].astype(BF16), p['proj_s5'].astype(BF16), p['proj_rwkv'].astype(BF16),
      p['w_out'].astype(BF16), p['norm2'].reshape(1, d), wr, br, tri)


DMA_UNROLL = 8


def _start_rows(n_rows, make_copy, sems):
    def start(g, c):
        for u in range(DMA_UNROLL):
            r = g * DMA_UNROLL + u
            for k in range(2):
                make_copy(r, k, sems.at[k]).start(priority=k)
        return c

    lax.fori_loop(0, n_rows // DMA_UNROLL, start, 0)


def _wait_rows(n_rows, make_copy, sems):
    def wait(g, c):
        for k in range(2):
            for u in range(DMA_UNROLL):
                make_copy(g * DMA_UNROLL + u, k, sems.at[k]).wait()
        return c

    lax.fori_loop(0, n_rows // DMA_UNROLL, wait, 0)


def _row_copies(n_rows, make_copy, sems):
    _start_rows(n_rows, make_copy, sems)
    _wait_rows(n_rows, make_copy, sems)


def _dispatch_kernel(pend_ref, h_ref, dest_ref, hs_ref, zero_ref, sems, zsem):
    tt = h_ref.shape[1]
    n_blocks = hs_ref.shape[0] // MOE_BLOCK
    first = (pl.program_id(0) == 0) & (pl.program_id(1) == 0)

    def zero_copy(start):
        return pltpu.make_async_copy(zero_ref, hs_ref.at[pl.ds(pl.multiple_of(start, MOE_BLOCK), MOE_BLOCK), :],
                                     zsem)

    @pl.when(first)
    def _():
        zero_ref[...] = jnp.zeros_like(zero_ref)
        for e in range(N_EXPERTS):
            @pl.when(pend_ref[e + 1] > pend_ref[e])
            def _():
                zero_copy(pend_ref[e + 1] - MOE_BLOCK).start()
        for e in range(N_EXPERTS):
            @pl.when(pend_ref[e + 1] > pend_ref[e])
            def _():
                zero_copy(pend_ref[e + 1] - MOE_BLOCK).wait()

        def clear(blk, c):
            cp = zero_copy(blk * MOE_BLOCK)
            cp.start()
            cp.wait()
            return c

        lax.fori_loop(pend_ref[N_EXPERTS] // MOE_BLOCK, n_blocks, clear, 0)

    def row_copy(r, k, sem):
        return pltpu.make_async_copy(h_ref.at[0, pl.ds(r, 1), :],
                                     hs_ref.at[pl.ds(dest_ref[0, k, r], 1), :], sem)

    _row_copies(tt, row_copy, sems)


def moe_dispatch(h2, dest, pad_end0, n_slots, tt):
    b, t, d = h2.shape
    nsel = t // tt
    grid_spec = pltpu.PrefetchScalarGridSpec(
        num_scalar_prefetch=1,
        grid=(b, nsel),
        in_specs=[pl.BlockSpec((1, tt, d), lambda i, j, pe: (i, j, 0)),
                  pl.BlockSpec((1, 2, tt), lambda i, j, pe: (i * nsel + j, 0, 0),
                               memory_space=pltpu.SMEM)],
        out_specs=pl.BlockSpec(memory_space=pl.ANY),
        scratch_shapes=[pltpu.VMEM((MOE_BLOCK, d), F32),
                        pltpu.SemaphoreType.DMA((2,)), pltpu.SemaphoreType.DMA(())],
    )
    return pl.pallas_call(
        _dispatch_kernel,
        grid_spec=grid_spec,
        out_shape=jax.ShapeDtypeStruct((n_slots, d), F32),
        compiler_params=_cparams(2),
        name="moe_dispatch",
    )(pad_end0, h2, dest)


def _expert_kernel(be_ref, nu_ref, x_ref, wg_ref, wu_ref, wd_ref, o_ref):
    del be_ref
    used = pl.program_id(0) < nu_ref[0]

    @pl.when(used)
    def _():
        x = x_ref[...].astype(BF16)
        a = jnp.dot(x, wg_ref[0], preferred_element_type=F32)
        u = jnp.dot(x, wu_ref[0], preferred_element_type=F32)
        hmid = (a * _sigmoid(a) * u).astype(BF16)
        o_ref[...] = jnp.dot(hmid, wd_ref[0], preferred_element_type=F32)

    @pl.when(jnp.logical_not(used))
    def _():
        o_ref[...] = jnp.zeros_like(o_ref)


def moe_experts(hs, blk_expert, n_used, w_gate, w_up, w_down):
    n_slots, d = hs.shape
    f = w_gate.shape[2]
    nblk = n_slots // MOE_BLOCK
    rows = lambda i, be, nu: (jnp.minimum(i, nu[0] - 1), 0)
    wsel = lambda i, be, nu: (be[jnp.minimum(i, nu[0] - 1)], 0, 0)
    grid_spec = pltpu.PrefetchScalarGridSpec(
        num_scalar_prefetch=2,
        grid=(nblk,),
        in_specs=[pl.BlockSpec((MOE_BLOCK, d), rows),
                  pl.BlockSpec((1, d, f), wsel), pl.BlockSpec((1, d, f), wsel),
                  pl.BlockSpec((1, f, d), wsel)],
        out_specs=pl.BlockSpec((MOE_BLOCK, d), lambda i, be, nu: (i, 0)),
    )
    return pl.pallas_call(
        _expert_kernel,
        grid_spec=grid_spec,
        out_shape=jax.ShapeDtypeStruct((n_slots, d), F32),
        compiler_params=_cparams(1),
        name="moe_experts",
    )(blk_expert, n_used, hs, w_gate.astype(BF16), w_up.astype(BF16), w_down.astype(BF16))


def _combine_kernel(x_ref, mod_ref, dest_ref, dest_next_ref, gate_ref, ys_ref, o_ref, buf_ref, sems):
    tt = x_ref.shape[1]
    g = pl.program_id(0) * pl.num_programs(1) + pl.program_id(1)
    n_tiles = pl.num_programs(0) * pl.num_programs(1)
    slot = lax.rem(g, 2)

    def gather(dref, sl):
        def row_copy(r, k, sem):
            return pltpu.make_async_copy(ys_ref.at[pl.ds(dref[0, k, r], 1), :],
                                         buf_ref.at[sl, k, pl.ds(r, 1), :], sem)
        return row_copy

    @pl.when(g == 0)
    def _():
        _start_rows(tt, gather(dest_ref, 0), sems.at[0])

    @pl.when(g + 1 < n_tiles)
    def _():
        _start_rows(tt, gather(dest_next_ref, 1 - slot), sems.at[1 - slot])

    _wait_rows(tt, gather(dest_ref, slot), sems.at[slot])
    y = gate_ref[0, :, 0:1] * buf_ref[slot, 0] + gate_ref[0, :, 1:2] * buf_ref[slot, 1]
    o_ref[0] = x_ref[0] + mod_ref[0, 0, 5:6, :] * y


def moe_combine(x_mid, modsel, dest, gates_t, ys, tt, t0):
    b, t, d = x_mid.shape
    nsel = t // tt
    grid_spec = pltpu.PrefetchScalarGridSpec(
        num_scalar_prefetch=0,
        grid=(b, nsel),
        in_specs=[pl.BlockSpec((1, tt, d), lambda i, j: (i, j, 0)),
                  pl.BlockSpec((1, 1, 6, d), lambda i, j: (i, jnp.minimum(j + t0, 1), 0, 0)),
                  pl.BlockSpec((1, 2, tt), lambda i, j: (i * nsel + j, 0, 0), memory_space=pltpu.SMEM),
                  pl.BlockSpec((1, 2, tt), lambda i, j: (jnp.minimum(i * nsel + j + 1, b * nsel - 1), 0, 0),
                               memory_space=pltpu.SMEM),
                  pl.BlockSpec((1, tt, 2), lambda i, j: (i * nsel + j, 0, 0)),
                  pl.BlockSpec(memory_space=pl.ANY)],
        out_specs=pl.BlockSpec((1, tt, d), lambda i, j: (i, j, 0)),
        scratch_shapes=[pltpu.VMEM((2, 2, tt, d), F32), pltpu.SemaphoreType.DMA((2, 2))],
    )
    return pl.pallas_call(
        _combine_kernel,
        grid_spec=grid_spec,
        out_shape=jax.ShapeDtypeStruct((b, t, d), F32),
        compiler_params=_cparams(2),
        name="moe_combine",
    )(x_mid, modsel, dest, dest, gates_t, ys)


def moe_layer(x_mid, h2, ri, rg, cnt, modsel, p, tt, t0):
    b, t, d = x_mid.shape
    nsel = t // tt
    n_tok = b * nsel * tt
    n_blocks = -(-(n_tok * 2) // MOE_BLOCK) + N_EXPERTS
    n_slots = n_blocks * MOE_BLOCK
    counts = cnt[:, 0].astype(jnp.int32)
    padded = (counts + MOE_BLOCK - 1) // MOE_BLOCK * MOE_BLOCK
    pad_end = jnp.cumsum(padded)
    pad_start = pad_end - padded
    dest = pad_start[ri[:, 0:2, :]] + ri[:, 2:4, :]
    blk_start = jnp.arange(n_blocks, dtype=jnp.int32) * MOE_BLOCK
    blk_expert = jnp.minimum(jnp.sum(blk_start[:, None] >= pad_end[None, :], axis=1),
                             N_EXPERTS - 1).astype(jnp.int32)
    n_used = (pad_end[-1:] // MOE_BLOCK).astype(jnp.int32)
    pad_end0 = jnp.concatenate([jnp.zeros((1,), jnp.int32), pad_end.astype(jnp.int32)])
    hs = moe_dispatch(h2, dest, pad_end0, n_slots, tt)
    ys = moe_experts(hs, blk_expert, n_used, p['exp_w_gate'], p['exp_w_up'], p['exp_w_down'])
    gates_t = jnp.transpose(rg[:, 0:2, :], (0, 2, 1))
    return moe_combine(x_mid, modsel, dest, gates_t, ys, tt, t0)


def kernel(x, c, ctx, c_ctx, w_mod, b_mod, norm1, w_in, q_gain, k_gain, s5_a_re, s5_a_im, s5_log_dt, s5_b_re, s5_b_im, s5_c_re, s5_c_im, s5_d, s5_w_glu, rwkv_conv, rwkv_w0, rwkv_w2, rwkv_a0, rwkv_a2, rwkv_g2, rwkv_k_k, rwkv_k_a, rwkv_r_k, rwkv_ln_w, rwkv_ln_b, proj_att, proj_s5, proj_rwkv, w_out, norm2, router_g_w, router_g_b, router_e_w, router_e_b, exp_w_gate, exp_w_up, exp_w_down):
    b, l_len, d = x.shape
    c_len = ctx.shape[1]
    depth = w_mod.shape[0]
    t = c_len + l_len
    tt = 256 if c_len % 256 == 0 else c_len
    assert l_len % tt == 0 and tt % S5_CHUNK == 0 and tt % RW_TB == 0 and tt % DMA_UNROLL == 0
    assert b * RW_H == LANES, "the recurrence puts (batch, head) on the 128 lanes"

    rows = -(-(b + 1) // 8) * 8
    cc = jnp.concatenate([c, c_ctx[None], jnp.zeros((rows - b - 1, d), F32)], axis=0)
    mods = modulation(cc, w_mod, b_mod)
    cos_t, sin_t = rope_tables(c_len, l_len)

    xa = jnp.concatenate([ctx, x], axis=1)
    for l in range(depth):
        need_ctx = l < depth - 1
        t0 = 0 if need_ctx else c_len // tt
        mod_lat = mods[l, :b].reshape(b, 1, 6, d)
        mod_ctx = jnp.broadcast_to(mods[l, b].reshape(1, 1, 6, d), (b, 1, 6, d))
        modsel = jnp.concatenate([mod_ctx, mod_lat], axis=1)

        a, s5u, z, gl = in_projection(xa, modsel, norm1[l], w_in[l], tt, 0)
        o_att = attention(a, cos_t, sin_t, q_gain[l], k_gain[l], tt, t0, c_len)
        ops = s5_operators(s5_a_re[l], s5_a_im[l], s5_log_dt[l], s5_b_re[l], s5_b_im[l],
                           s5_c_re[l], s5_c_im[l])
        y_s5 = s5_scan(s5u, ops, c_len)
        rws, streams = rwkv_prepare(z, rwkv_conv[l], rwkv_w0[l], rwkv_w2[l], rwkv_a0[l], rwkv_a2[l],
                                    rwkv_g2[l], rwkv_k_k[l], rwkv_k_a[l], rwkv_r_k[l], tt, c_len)
        y_rw = rwkv_scan(streams, b, c_len)
        p = dict(s5_d=s5_d[l], s5_w_glu=s5_w_glu[l], rwkv_ln_w=rwkv_ln_w[l], rwkv_ln_b=rwkv_ln_b[l],
                 proj_att=proj_att[l], proj_s5=proj_s5[l], proj_rwkv=proj_rwkv[l], w_out=w_out[l],
                 norm2=norm2[l], router_g_w=router_g_w[l], router_g_b=router_g_b[l],
                 router_e_w=router_e_w[l], router_e_b=router_e_b[l],
                 exp_w_gate=exp_w_gate[l], exp_w_up=exp_w_up[l], exp_w_down=exp_w_down[l])
        x_mid, h2, ri, rg, cnt = merge_and_route(xa, modsel, gl, o_att, y_s5, s5u, y_rw, rws, p, tt, t0)
        xa = moe_layer(x_mid, h2, ri, rg, cnt, modsel, p, tt, t0)
    return xa
```

```python
import functools

import jax
import jax.numpy as jnp
from jax import lax
from jax.experimental import pallas as pl
from jax.experimental.pallas import tpu as pltpu

F32 = jnp.float32
BF16 = jnp.bfloat16
HIGHEST = lax.Precision.HIGHEST

EPS = 1e-6
GRID_W = 64
ROPE_THETA = 10000.0

HEAD_DIM = 64
ATT_W = 512
KV_W = 256
ATT_IN = ATT_W + 2 * KV_W

S5_W = 256
S5_CG = 16
S5_G = 16
S5_N = 64
S5_CHUNK = 16

RW_W = 256
RW_N = 64
RW_H = 4
LORA = 64
GATE_LORA = 128
RW_STREAM = 3 * RW_W + 2 * LORA + GATE_LORA
GN_EPS = 64e-5
RW_TB = 16
RW_IQ = 16

N_BRANCH = 3
N_GROUPS = 4
EPG = 8
N_EXPERTS = 32
MOE_BLOCK = 256
ROUTE_ROWS = 40

LANES = 128
VMEM_LIMIT = 56 * 1024 * 1024


def _cparams(n_axes, vmem=VMEM_LIMIT):
    return pltpu.CompilerParams(dimension_semantics=("arbitrary",) * n_axes,
                                vmem_limit_bytes=vmem)


def _sigmoid(x):
    return 1.0 / (1.0 + jnp.exp(-x))


def _seg64_sum(x):
    parts = []
    for s in range(x.shape[-1] // LANES):
        xs = x[:, s * LANES:(s + 1) * LANES]
        lo = lax.broadcasted_iota(jnp.int32, xs.shape, 1) < 64
        s_lo = jnp.sum(jnp.where(lo, xs, 0.0), axis=-1, keepdims=True)
        s_hi = jnp.sum(jnp.where(lo, 0.0, xs), axis=-1, keepdims=True)
        parts.append(jnp.where(lo, s_lo, s_hi))
    return parts[0] if len(parts) == 1 else jnp.concatenate(parts, axis=-1)


def _mod_kernel(c_ref, w_ref, b_ref, o_ref):
    c = c_ref[...]
    s = c * _sigmoid(c)
    o_ref[0] = jnp.dot(s, w_ref[0], precision=HIGHEST, preferred_element_type=F32) + b_ref[0]


def modulation(cc, w_mod, b_mod):
    depth, d, n = w_mod.shape
    rows = cc.shape[0]
    nt = n // 4
    return pl.pallas_call(
        _mod_kernel,
        grid=(depth, n // nt),
        in_specs=[pl.BlockSpec((rows, d), lambda l, j: (0, 0)),
                  pl.BlockSpec((1, d, nt), lambda l, j: (l, 0, j)),
                  pl.BlockSpec((1, 1, nt), lambda l, j: (l, 0, j))],
        out_specs=pl.BlockSpec((1, rows, nt), lambda l, j: (l, 0, j)),
        out_shape=jax.ShapeDtypeStruct((depth, rows, n), F32),
        compiler_params=_cparams(2),
        name="modulation",
    )(cc, w_mod, b_mod.reshape(depth, 1, n))


def _residual_specs(res, tt, t0):
    head, tail, n_head = res
    d = head.shape[2]
    return [pl.BlockSpec((1, tt, d), lambda i, j: (i, jnp.minimum(j + t0, max(n_head - 1, 0)), 0)),
            pl.BlockSpec((1, tt, d), lambda i, j: (i, jnp.maximum(j + t0 - n_head, 0), 0))]


def _residual_tile(xh_ref, xt_ref, n_head, t0):
    if n_head == 0:
        return xt_ref[0]
    return jnp.where(pl.program_id(1) + t0 < n_head, xh_ref[0], xt_ref[0])


def _inproj_kernel(xh_ref, xt_ref, mod_ref, n1_ref, wa_ref, ws_ref, wr_ref, wg_ref,
                   a_ref, s_ref, z_ref, g_ref, *, n_head):
    x = _residual_tile(xh_ref, xt_ref, n_head, 0)
    ms = jnp.mean(x * x, axis=-1, keepdims=True)
    y = x * lax.rsqrt(ms + EPS) * n1_ref[...]
    h = (y * (1.0 + mod_ref[0, 0, 1:2, :]) + mod_ref[0, 0, 0:1, :]).astype(BF16)
    a_ref[0] = jnp.dot(h, wa_ref[...], preferred_element_type=F32).astype(BF16)
    s_ref[0] = jnp.dot(h, ws_ref[...], preferred_element_type=F32)
    z_ref[0] = jnp.dot(h, wr_ref[...], preferred_element_type=F32)
    g_ref[0] = jnp.dot(h, wg_ref[...], preferred_element_type=F32).astype(BF16)


def in_projection(res, t, modsel, norm1, w_in, tt):
    b, _, d = res[0].shape
    t0 = 0
    ntile = t // tt
    off_s5 = ATT_IN
    off_rw = off_s5 + S5_W
    off_gate = off_rw + RW_STREAM
    wa = w_in[:, :off_s5].astype(BF16)
    ws = w_in[:, off_s5:off_rw].astype(BF16)
    wr = w_in[:, off_rw:off_gate].astype(BF16)
    wg = w_in[:, off_gate:].astype(BF16)
    tok = lambda w: pl.BlockSpec((1, tt, w), lambda i, j: (i, j + t0, 0))
    full = lambda w: pl.BlockSpec((d, w), lambda i, j: (0, 0))
    return pl.pallas_call(
        functools.partial(_inproj_kernel, n_head=res[2]),
        grid=(b, ntile - t0),
        in_specs=_residual_specs(res, tt, t0)
        + [pl.BlockSpec((1, 1, 6, d), lambda i, j: (i, jnp.minimum(j + t0, 1), 0, 0)),
                  pl.BlockSpec((1, d), lambda i, j: (0, 0)),
                  full(ATT_IN), full(S5_W), full(RW_STREAM), full(N_BRANCH * d)],
        out_specs=[tok(ATT_IN), tok(S5_W), tok(RW_STREAM), tok(N_BRANCH * d)],
        out_shape=[jax.ShapeDtypeStruct((b, t, ATT_IN), BF16),
                   jax.ShapeDtypeStruct((b, t, S5_W), F32),
                   jax.ShapeDtypeStruct((b, t, RW_STREAM), F32),
                   jax.ShapeDtypeStruct((b, t, N_BRANCH * d), BF16)],
        compiler_params=_cparams(2),
        name="in_projection",
    )(res[0], res[1], modsel, norm1.reshape(1, d), wa, ws, wr, wg)


def _headnorm(x, gain):
    ms = _seg64_sum(x * x) * (1.0 / HEAD_DIM)
    return x * lax.rsqrt(ms + EPS) * gain


def _rope(x, cos, sin):
    lane = lax.broadcasted_iota(jnp.int32, x.shape, 1)
    first = (lane & 16) == 0
    partner = jnp.where(first, pltpu.roll(x, LANES - 16, 1), pltpu.roll(x, 16, 1))
    return x * cos + partner * sin


def _attend_slab(q, hh, kn, vmt):
    lane = lax.broadcasted_iota(jnp.int32, q.shape, 1)
    kv_lanes = (lane >= 64) if hh == 1 else (lane < 64)
    q_sw = pltpu.roll(q, 64, 1)
    srcs = (q, q_sw) if hh == 0 else (q_sw, q)
    rows = q.shape[0]
    qm = jnp.concatenate([jnp.where(kv_lanes, src, 0.0) for src in srcs], axis=0).astype(BF16)
    st = lax.dot_general(kn, qm, (((1,), (1,)), ((), ())), preferred_element_type=F32)
    m = jnp.max(st, axis=0, keepdims=True)
    et = jnp.exp((st - m).astype(BF16))
    pvt = jnp.dot(vmt, et, preferred_element_type=F32)
    pvt = pvt / pltpu.roll(pvt, 64, 0)
    own = slice(64 * hh, 64 * hh + 64)
    out_t = jnp.concatenate([pvt[own, :rows], pvt[own, rows:]], axis=0)
    return out_t.T


def _attn_kernel(q_ref, kv_ref, cq_ref, sq_ref, ck_ref, sk_ref, qg_ref, kg_ref,
                 o_ref, kn_ref, vm_ref, *, t0, c_len):
    j = pl.program_id(1)
    t_all = kv_ref.shape[1]

    @pl.when(j == 0)
    def _():
        for p in range(2):
            k = kv_ref[0, :, p * LANES:(p + 1) * LANES].astype(F32)
            k = _rope(_headnorm(k, kg_ref[...]), ck_ref[...], sk_ref[...])
            kn_ref[p] = k.astype(BF16)
            vt = kv_ref[0, :, KV_W + p * LANES:KV_W + (p + 1) * LANES].astype(F32).T
            low = lax.broadcasted_iota(jnp.int32, vt.shape, 0) < 64
            vm_ref[2 * p] = jnp.where(low, vt, 1.0).astype(BF16)
            vm_ref[2 * p + 1] = jnp.where(low, 1.0, vt).astype(BF16)

    def run(n_keys):
        for p in range(2):
            kn = kn_ref[p, 0:n_keys, :]
            for hh in range(2):
                slab = 2 * p + hh
                vp = vm_ref[slab, :, 0:n_keys]
                q = q_ref[0, :, slab * LANES:(slab + 1) * LANES].astype(F32)
                q = _rope(_headnorm(q, qg_ref[...]), cq_ref[...], sq_ref[...]) * (HEAD_DIM ** -0.5)
                o_ref[0, :, slab * LANES:(slab + 1) * LANES] = _attend_slab(q, hh, kn, vp).astype(BF16)

    if t0 == 0:
        pl.when(j == 0)(lambda: run(c_len))
        pl.when(j > 0)(lambda: run(t_all))
    else:
        run(t_all)


def attention(a, cos_t, sin_t, q_gain, k_gain, tt, t0, c_len):
    b, t, _ = a.shape
    ntile = t // tt
    qg = jnp.tile(q_gain, 2).reshape(1, LANES)
    kg = jnp.tile(k_gain, 2).reshape(1, LANES)
    tab_q = pl.BlockSpec((tt, LANES), lambda i, j: (j + t0, 0))
    tab_k = pl.BlockSpec((t, LANES), lambda i, j: (0, 0))
    vec = pl.BlockSpec((1, LANES), lambda i, j: (0, 0))
    return pl.pallas_call(
        functools.partial(_attn_kernel, t0=t0, c_len=c_len),
        grid=(b, ntile - t0),
        in_specs=[pl.BlockSpec((1, tt, ATT_W), lambda i, j: (i, j + t0, 0)),
                  pl.BlockSpec((1, t, ATT_W), lambda i, j: (i, 0, 1)),
                  tab_q, tab_q, tab_k, tab_k, vec, vec],
        out_specs=pl.BlockSpec((1, tt, ATT_W), lambda i, j: (i, j, 0)),
        out_shape=jax.ShapeDtypeStruct((b, t - t0 * tt, ATT_W), BF16),
        scratch_shapes=[pltpu.VMEM((2, t, LANES), BF16), pltpu.VMEM((4, LANES, t), BF16)],
        compiler_params=_cparams(2),
        name="attention",
    )(a, a, cos_t, sin_t, cos_t, sin_t, qg, kg)


def rope_tables(c_len, l_len):
    half = HEAD_DIM // 4
    n = jnp.arange(l_len, dtype=jnp.int32)
    inv = ROPE_THETA ** (-jnp.arange(half, dtype=F32) / half)
    ar = (n // GRID_W).astype(F32)[:, None] * inv[None, :]
    ac = (n % GRID_W).astype(F32)[:, None] * inv[None, :]
    cos = jnp.concatenate([jnp.cos(ar), jnp.cos(ar), jnp.cos(ac), jnp.cos(ac)], axis=-1)
    sin = jnp.concatenate([-jnp.sin(ar), jnp.sin(ar), -jnp.sin(ac), jnp.sin(ac)], axis=-1)
    cos = jnp.concatenate([jnp.ones((c_len, HEAD_DIM), F32), cos], axis=0)
    sin = jnp.concatenate([jnp.zeros((c_len, HEAD_DIM), F32), sin], axis=0)
    return jnp.tile(cos, (1, 2)), jnp.tile(sin, (1, 2))


def _s5_kernel(x_ref, mt_ref, qt_ref, pt_ref, ar_ref, ai_ref, y_ref, hloc_ref, hs_ref, *, nb, n_ctx):
    d = pl.program_id(0)
    x = x_ref[0]
    hloc_ref[...] = jnp.dot(x, qt_ref[0, 0], preferred_element_type=F32)
    n_chunks = x.shape[0] // nb
    a_r = ar_ref[0, 0]
    a_i = ai_ref[0, 0]

    def body(k, h):
        k_rev = jnp.where(k < n_ctx, n_ctx - 1 - k, n_chunks - 1 - (k - n_ctx))
        c = jnp.where(d == 0, k, k_rev)
        rows = pl.ds(pl.multiple_of(c * nb, nb), nb)
        hs_ref[rows, :] = h
        return h * a_r + pltpu.roll(h, S5_N, 1) * a_i + hloc_ref[rows, :]

    lax.fori_loop(0, n_chunks, body, jnp.zeros((nb, 2 * S5_N), F32))
    y_ref[0, 0] = (jnp.dot(x, mt_ref[0, 0], preferred_element_type=F32)
                   + jnp.dot(hs_ref[...].astype(BF16), pt_ref[0, 0], preferred_element_type=F32)
                   ).astype(y_ref.dtype)


def s5_operators(a_re, a_im, log_dt, b_re, b_im, c_re, c_im):
    tc = S5_CHUNK
    dt = jnp.exp(log_dt.astype(F32))[..., None]
    lam_re = a_re.astype(F32)
    lam_im = a_im.astype(F32)
    mag = jnp.exp(lam_re * dt)
    ab_re = mag * jnp.cos(lam_im * dt)
    ab_im = mag * jnp.sin(lam_im * dt)
    den = lam_re * lam_re + lam_im * lam_im
    nr = ab_re - 1.0
    z_re = (nr * lam_re + ab_im * lam_im) / den
    z_im = (ab_im * lam_re - nr * lam_im) / den
    br = b_re.astype(F32)
    bi = b_im.astype(F32)
    bb_re = z_re[..., None] * br - z_im[..., None] * bi
    bb_im = z_re[..., None] * bi + z_im[..., None] * br
    k = jnp.arange(tc + 1, dtype=F32)[:, None, None, None]
    pw_mag = jnp.exp(k * (lam_re * dt)[None])
    pw_re = pw_mag * jnp.cos(k * (lam_im * dt)[None])
    pw_im = pw_mag * jnp.sin(k * (lam_im * dt)[None])
    cr = c_re.astype(F32)
    ci = c_im.astype(F32)
    ca_re = cr[None] * pw_re[:, :, :, None, :] - ci[None] * pw_im[:, :, :, None, :]
    ca_im = cr[None] * pw_im[:, :, :, None, :] + ci[None] * pw_re[:, :, :, None, :]
    kern = (jnp.einsum('kdgcn,dgne->kdgce', ca_re[:tc], bb_re, precision=HIGHEST)
            - jnp.einsum('kdgcn,dgne->kdgce', ca_im[:tc], bb_im, precision=HIGHEST))
    s_idx = jnp.arange(tc)[:, None]
    t_idx = jnp.arange(tc)[None, :]
    mts, pts, qts = [], [], []
    for d in range(2):
        lag = (t_idx - s_idx) if d == 0 else (s_idx - t_idx)
        toe = kern[:, d][jnp.clip(lag, 0, tc - 1)]
        toe = jnp.where((lag >= 0)[:, :, None, None, None], toe, 0.0)
        mts.append(jnp.transpose(toe, (2, 0, 4, 1, 3)).reshape(S5_G, tc * S5_CG, tc * S5_CG))
        steps = slice(1, tc + 1) if d == 0 else slice(tc, 0, -1)
        p_re = jnp.transpose(ca_re[steps, d], (1, 3, 0, 2))
        p_im = jnp.transpose(ca_im[steps, d], (1, 3, 0, 2))
        pts.append(jnp.concatenate([p_re, -p_im], axis=1).reshape(S5_G, 2 * S5_N, tc * S5_CG))
        order = slice(tc - 1, None, -1) if d == 0 else slice(0, tc)
        rev_re = pw_re[order, d]
        rev_im = pw_im[order, d]
        q_re = rev_re[..., None] * bb_re[d][None] - rev_im[..., None] * bb_im[d][None]
        q_im = rev_re[..., None] * bb_im[d][None] + rev_im[..., None] * bb_re[d][None]
        q_re = jnp.transpose(q_re, (1, 0, 3, 2))
        q_im = jnp.transpose(q_im, (1, 0, 3, 2))
        qts.append(jnp.concatenate([q_re, q_im], axis=-1).reshape(S5_G, tc * S5_CG, 2 * S5_N))
    a_row = jnp.concatenate([pw_re[tc], pw_re[tc]], axis=-1)[:, :, None, :]
    ai_row = jnp.concatenate([-pw_im[tc], pw_im[tc]], axis=-1)[:, :, None, :]
    return (jnp.stack(mts).astype(BF16), jnp.stack(qts).astype(BF16), jnp.stack(pts).astype(BF16),
            a_row, ai_row)


def s5_scan(s5u, ops, c_len):
    b, t, _ = s5u.shape
    tc = S5_CHUNK
    nch = t // tc
    mt, qt, pt, a_row, ai_row = ops
    x = s5u.reshape(b, nch, tc, S5_G, S5_CG)
    x = jnp.transpose(x, (3, 1, 0, 2, 4)).reshape(S5_G, nch * b, tc * S5_CG).astype(BF16)
    r = nch * b
    w = tc * S5_CG
    spec = lambda s0, s1: pl.BlockSpec((1, 1, s0, s1), lambda d, g: (d, g, 0, 0))
    y = pl.pallas_call(
        functools.partial(_s5_kernel, nb=b, n_ctx=c_len // tc),
        grid=(2, S5_G),
        in_specs=[pl.BlockSpec((1, r, w), lambda d, g: (g, 0, 0)),
                  spec(w, w), spec(w, 2 * S5_N), spec(2 * S5_N, w),
                  spec(1, 2 * S5_N), spec(1, 2 * S5_N)],
        out_specs=spec(r, w),
        out_shape=jax.ShapeDtypeStruct((2, S5_G, r, w), BF16),
        scratch_shapes=[pltpu.VMEM((r, 2 * S5_N), F32), pltpu.VMEM((r, 2 * S5_N), F32)],
        compiler_params=_cparams(2),
        name="s5_scan",
    )(x, mt, qt, pt, a_row, ai_row)
    y = y.reshape(2, S5_G, nch, b, tc, S5_CG)
    return jnp.transpose(y, (0, 3, 2, 4, 1, 5)).reshape(2, b, t, S5_W)


RW_STREAMS = 9


def _rwprep_kernel(z_ref, conv_ref, w0_ref, w2_ref, a0_ref, a2_ref, g2_ref, kk_ref, ka_ref, rk_ref,
                   fin_ref, *stream_refs, tt, c_len):
    t_all = z_ref.shape[1]
    r0 = pl.multiple_of(pl.program_id(1) * tt, tt)
    zc = z_ref[0, pl.ds(r0, tt), :]
    row = lax.broadcasted_iota(jnp.int32, zc.shape, 0)
    has_prev = (r0 != 0) & (r0 != c_len)
    has_next = (r0 + tt != c_len) & (r0 + tt != t_all)
    prev_row = z_ref[0, pl.ds(jnp.maximum(r0 - 1, 0), 1), :]
    next_row = z_ref[0, pl.ds(jnp.minimum(r0 + tt, t_all - 1), 1), :]
    prev_row = jnp.where(has_prev, prev_row, 0.0)
    next_row = jnp.where(has_next, next_row, 0.0)
    zp = jnp.where(row == 0, prev_row, pltpu.roll(zc, 1, 0))
    zn = jnp.where(row == tt - 1, next_row, pltpu.roll(zc, tt - 1, 0))
    zz = zp * conv_ref[0:1, :] + zc * conv_ref[1:2, :] + zn * conv_ref[2:3, :]
    r = zz[:, 0:RW_W]
    k = zz[:, RW_W:2 * RW_W]
    v = zz[:, 2 * RW_W:3 * RW_W]
    lora = zz[:, 3 * RW_W:3 * RW_W + 2 * LORA]
    xg = zz[:, 3 * RW_W + 2 * LORA:]
    kk = k * kk_ref[...]
    kk = kk * lax.rsqrt(_seg64_sum(kk * kk) + 1e-12)
    lora_w = jnp.tanh(lora).astype(BF16)
    lora_a = lora.astype(BF16)
    outs = [r, v, kk]
    kt_sum = None
    for d in range(2):
        wl = w0_ref[d:d + 1, :] + jnp.dot(lora_w, w2_ref[d], preferred_element_type=F32)
        sp = jnp.maximum(-wl, 0.0) + jnp.log(1.0 + jnp.exp(-jnp.abs(wl)))
        decay = jnp.exp(-jnp.exp(-sp - 0.5))
        a = _sigmoid(a0_ref[d:d + 1, :] + jnp.dot(lora_a, a2_ref[d], preferred_element_type=F32))
        kt = k * (1.0 + (a - 1.0) * ka_ref[...])
        outs += [decay, kk * a, kt]
        kt_sum = kt if kt_sum is None else kt_sum + kt
    bonus = _seg64_sum(r * kt_sum * rk_ref[...])
    gate = jnp.dot(_sigmoid(xg).astype(BF16), g2_ref[...], preferred_element_type=F32)
    fin_ref[0, :, 0:RW_W] = bonus * v
    fin_ref[0, :, RW_W:2 * RW_W] = gate
    for ref, val in zip(stream_refs, outs):
        ref[...] = val


def rwkv_prepare(z, conv_w, w0, w2, a0, a2, g2, k_k, k_a, r_k, tt, c_len):
    b, t, _ = z.shape
    zeros = jnp.zeros((2, LORA, RW_W), F32)
    w2p = jnp.concatenate([w2, zeros], axis=1).astype(BF16)
    a2p = jnp.concatenate([zeros, a2], axis=1).astype(BF16)
    row = lambda v: v.reshape(1, RW_W)
    c2 = lambda s0, s1: pl.BlockSpec((s0, s1), lambda i, j: (0, 0))
    c3 = lambda s0, s1, s2: pl.BlockSpec((s0, s1, s2), lambda i, j: (0, 0, 0))
    outs = pl.pallas_call(
        functools.partial(_rwprep_kernel, tt=tt, c_len=c_len),
        grid=(b, t // tt),
        in_specs=[pl.BlockSpec((1, t, RW_STREAM), lambda i, j: (i, 0, 0)),
                  c2(3, RW_STREAM), c2(2, RW_W), c3(2, 2 * LORA, RW_W), c2(2, RW_W),
                  c3(2, 2 * LORA, RW_W), c2(GATE_LORA, RW_W), c2(1, RW_W), c2(1, RW_W), c2(1, RW_W)],
        out_specs=[pl.BlockSpec((1, tt, 2 * RW_W), lambda i, j: (i, j, 0))]
        + [pl.BlockSpec((tt, RW_W), lambda i, j: (j, i))] * RW_STREAMS,
        out_shape=[jax.ShapeDtypeStruct((b, t, 2 * RW_W), F32)]
        + [jax.ShapeDtypeStruct((t, b * RW_W), F32)] * RW_STREAMS,
        compiler_params=_cparams(2),
        name="rwkv_prepare",
    )(z, conv_w, w0, w2p, a0, a2p, g2.astype(BF16), row(k_k), row(k_a), row(r_k))
    return outs[0], outs[1:]


RW_DIR_INPUTS = 7


def _key_row(ref, lead, j):
    return ref[(*lead, pl.ds(j, RW_IQ, stride=0), slice(None))]


def _rwscan_kernel(*refs):
    ins = refs[:2 * RW_DIR_INPUTS]
    y_refs = refs[2 * RW_DIR_INPUTS:2 * RW_DIR_INPUTS + 2]
    s_ref, sa_ref, kkx_ref, wc_ref, rows_ref = refs[2 * RW_DIR_INPUTS + 2:]
    tb = y_refs[0].shape[0]

    @pl.when(pl.program_id(0) == 0)
    def _():
        s_ref[...] = jnp.zeros_like(s_ref)
        sa_ref[...] = jnp.zeros_like(sa_ref)
        wc_ref[...] = jnp.ones_like(wc_ref)

    kkx_ref[0, 0:tb] = ins[5][...]
    kkx_ref[0, tb:tb + 1] = ins[6][0:1]
    kkx_ref[1, 1:tb + 1] = ins[RW_DIR_INPUTS + 5][...]
    kkx_ref[1, 0:1] = ins[RW_DIR_INPUTS + 6][tb - 1:tb]

    def step(tt, block_start):
        for d in range(2):
            w_ref, kka_ref, kt_ref, r_ref, v_ref = ins[RW_DIR_INPUTS * d:RW_DIR_INPUTS * d + 5]
            row = tt if d == 0 else tb - 1 - tt
            nxt = tt + 1 if d == 0 else tb - 1 - tt
            if block_start:
                rows_ref[d, 4] = wc_ref[d]
                wc = w_ref[row]
            else:
                wc = wc_ref[d] * w_ref[row]
            wc_ref[d] = wc
            inv = 1.0 / wc
            rows_ref[d, 0] = kka_ref[row] * inv
            rows_ref[d, 1] = kt_ref[row] * inv
            rows_ref[d, 2] = kkx_ref[d, nxt] * wc
            rows_ref[d, 3] = r_ref[row] * wc

            for q in range(RW_N // RW_IQ):
                rows = slice(q * RW_IQ, (q + 1) * RW_IQ)
                sa = sa_ref[d, rows, :]
                v = v_ref[row, rows, :]
                acc_sa = jnp.zeros_like(sa)
                acc_y = jnp.zeros_like(sa)
                for j in range(RW_N):
                    s_old = s_ref[j, d, rows, :]
                    if block_start:
                        s_old = s_old * _key_row(rows_ref, (d, 4), j)
                    s_new = s_old - sa * _key_row(rows_ref, (d, 0), j) + v * _key_row(rows_ref, (d, 1), j)
                    s_ref[j, d, rows, :] = s_new
                    acc_sa = acc_sa + s_new * _key_row(rows_ref, (d, 2), j)
                    acc_y = acc_y + s_new * _key_row(rows_ref, (d, 3), j)
                sa_ref[d, rows, :] = acc_sa
                y_refs[d][row, rows, :] = acc_y

    step(0, True)

    def body(tt, carry):
        step(tt, False)
        return carry

    lax.fori_loop(1, tb, body, 0)


def rwkv_scan(streams, b, c_len):
    t = streams[0].shape[0]
    ni = b * RW_H
    n_blk = t // RW_TB
    n_ctx = c_len // RW_TB
    r, v, kk, w0, kka0, kt0, w1, kka1, kt1 = [
        jnp.swapaxes(s.reshape(t, ni, RW_N), 1, 2) for s in streams]

    fwd = lambda i: i
    rev = lambda i: jnp.where(i < n_ctx, n_ctx - 1 - i, n_blk - 1 - (i - n_ctx))
    after = lambda i: jnp.minimum(i + 1, n_blk - 1)
    spec = lambda f: pl.BlockSpec((RW_TB, RW_N, ni), lambda i: (f(i), 0, 0))
    dir_specs = lambda f: [spec(f)] * 6 + [spec(lambda i: f(after(i)))]
    y0, y1 = pl.pallas_call(
        _rwscan_kernel,
        grid=(n_blk,),
        in_specs=dir_specs(fwd) + dir_specs(rev),
        out_specs=[spec(fwd), spec(rev)],
        out_shape=[jax.ShapeDtypeStruct((t, RW_N, ni), F32)] * 2,
        scratch_shapes=[pltpu.VMEM((RW_N, 2, RW_N, ni), F32), pltpu.VMEM((2, RW_N, ni), F32),
                        pltpu.VMEM((2, RW_TB + 1, RW_N, ni), F32), pltpu.VMEM((2, RW_N, ni), F32),
                        pltpu.VMEM((2, 5, RW_N, ni), F32)],
        compiler_params=_cparams(1),
        name="rwkv_scan",
    )(w0, kka0, kt0, r, v, kk, kk, w1, kka1, kt1, r, v, kk, kk)
    back = lambda y: jnp.swapaxes(y, 1, 2).reshape(t, b * RW_W)
    return back(y0), back(y1)


def _merge_kernel(xh_ref, xt_ref, mod_ref, gl_ref, oatt_ref, ys5_ref, s5u_ref, yrw0_ref, yrw1_ref, rws_ref,
                  dskip_ref, wglu_ref, lnw_ref, lnb_ref, patt_ref, ps5_ref, prw_ref, wout_ref,
                  n2_ref, wr_ref, br_ref, tri_ref,
                  xo_ref, h2_ref, ri_ref, rg_ref, cnt_ref, base_ref, *, n_head, t0):
    d = xt_ref.shape[2]
    first = (pl.program_id(0) == 0) & (pl.program_id(1) == 0)

    @pl.when(first)
    def _():
        base_ref[...] = jnp.zeros_like(base_ref)

    y = ys5_ref[0, 0].astype(F32) + ys5_ref[1, 0].astype(F32) + dskip_ref[...] * s5u_ref[0]
    zg = jnp.dot(jax.nn.gelu(y).astype(BF16), wglu_ref[...], preferred_element_type=F32)
    o_s5 = zg[:, :S5_W] * _sigmoid(zg[:, S5_W:])

    yr = yrw0_ref[...] + yrw1_ref[...]
    mu = _seg64_sum(yr) * (1.0 / RW_N)
    yc = yr - mu
    var = _seg64_sum(yc * yc) * (1.0 / RW_N)
    yn = yc * lax.rsqrt(var + GN_EPS) * lnw_ref[...] + lnb_ref[...]
    o_rw = (yn + rws_ref[0, :, 0:RW_W]) * rws_ref[0, :, RW_W:2 * RW_W]

    gates = _sigmoid(gl_ref[0].astype(F32))
    merged = (gates[:, 0:d] * jnp.dot(oatt_ref[0], patt_ref[...], preferred_element_type=F32)
              + gates[:, d:2 * d] * jnp.dot(o_s5.astype(BF16), ps5_ref[...], preferred_element_type=F32)
              + gates[:, 2 * d:3 * d] * jnp.dot(o_rw.astype(BF16), prw_ref[...], preferred_element_type=F32))
    upd = jnp.dot(merged.astype(BF16), wout_ref[...], preferred_element_type=F32)
    x_new = _residual_tile(xh_ref, xt_ref, n_head, t0) + mod_ref[0, 0, 2:3, :] * upd
    xo_ref[0] = x_new

    ms = jnp.mean(x_new * x_new, axis=-1, keepdims=True)
    h2 = (x_new * lax.rsqrt(ms + EPS) * n2_ref[...]) * (1.0 + mod_ref[0, 0, 4:5, :]) + mod_ref[0, 0, 3:4, :]
    h2_ref[0] = h2

    logits = lax.dot_general(wr_ref[...], h2, (((1,), (1,)), ((), ())),
                             precision=HIGHEST, preferred_element_type=F32) + br_ref[...]
    gl4 = logits[N_EXPERTS:N_EXPERTS + N_GROUPS]
    grow = lax.broadcasted_iota(jnp.int32, gl4.shape, 0)
    gmax = jnp.max(gl4, axis=0, keepdims=True)
    gsel = jnp.min(jnp.where(gl4 == gmax, grow, N_GROUPS), axis=0, keepdims=True)
    g_w = 1.0 / jnp.sum(jnp.exp(gl4 - gmax), axis=0, keepdims=True)
    el = logits[0:N_EXPERTS]
    erow = lax.broadcasted_iota(jnp.int32, el.shape, 0)
    neg = jnp.float32(-1e30)
    cand = jnp.where((erow >> 3) == gsel, el, neg)
    m1 = jnp.max(cand, axis=0, keepdims=True)
    i1 = jnp.min(jnp.where(cand == m1, erow, N_EXPERTS), axis=0, keepdims=True)
    cand2 = jnp.where(erow == i1, neg, cand)
    m2 = jnp.max(cand2, axis=0, keepdims=True)
    i2 = jnp.min(jnp.where(cand2 == m2, erow, N_EXPERTS), axis=0, keepdims=True)
    e21 = jnp.exp(m2 - m1)
    gate1 = g_w / (1.0 + e21)
    gate2 = g_w * e21 / (1.0 + e21)

    oh1 = (erow == i1).astype(F32)
    oh2 = (erow == i2).astype(F32)
    both = oh1 + oh2
    before = jnp.dot(both.astype(BF16), tri_ref[...], preferred_element_type=F32) + base_ref[...]
    rank1 = jnp.sum(oh1 * before, axis=0, keepdims=True)
    rank2 = jnp.sum(oh2 * before, axis=0, keepdims=True)
    base_ref[...] = base_ref[...] + jnp.sum(both, axis=1, keepdims=True)
    cnt_ref[...] = jnp.broadcast_to(base_ref[...], cnt_ref.shape)

    zi = jnp.zeros((4, i1.shape[1]), jnp.int32)
    ri_ref[0, 0:1, :] = i1
    ri_ref[0, 1:2, :] = i2
    ri_ref[0, 2:3, :] = rank1.astype(jnp.int32)
    ri_ref[0, 3:4, :] = rank2.astype(jnp.int32)
    ri_ref[0, 4:8, :] = zi
    rg_ref[0, 0:1, :] = gate1
    rg_ref[0, 1:2, :] = gate2
    rg_ref[0, 2:8, :] = jnp.zeros((6, i1.shape[1]), F32)


def merge_and_route(res, t, modsel, gl, o_att, y_s5, s5u, y_rw, rws, p, tt, t0):
    b, _, d = res[0].shape
    ntile = t // tt
    nsel = ntile - t0
    tok = lambda w: pl.BlockSpec((1, tt, w), lambda i, j: (i, j + t0, 0))
    sel = lambda w: pl.BlockSpec((1, tt, w), lambda i, j: (i, j, 0))
    tok2 = lambda w: pl.BlockSpec((2, 1, tt, w), lambda i, j: (0, i, j + t0, 0))
    tmaj = lambda w: pl.BlockSpec((tt, w), lambda i, j: (j + t0, i))
    c2 = lambda s0, s1: pl.BlockSpec((s0, s1), lambda i, j: (0, 0))
    wr = jnp.concatenate([p['router_e_w'].T, p['router_g_w'].T,
                          jnp.zeros((ROUTE_ROWS - N_EXPERTS - N_GROUPS, d), F32)], axis=0)
    br = jnp.concatenate([p['router_e_b'], p['router_g_b'],
                          jnp.zeros((ROUTE_ROWS - N_EXPERTS - N_GROUPS,), F32)]).reshape(ROUTE_ROWS, 1)
    tri = (jnp.arange(tt)[:, None] < jnp.arange(tt)[None, :]).astype(BF16)
    route_spec = pl.BlockSpec((1, 8, tt), lambda i, j: (i * nsel + j, 0, 0))
    return pl.pallas_call(
        functools.partial(_merge_kernel, n_head=res[2], t0=t0),
        grid=(b, nsel),
        in_specs=_residual_specs(res, tt, t0)
        + [pl.BlockSpec((1, 1, 6, d), lambda i, j: (i, jnp.minimum(j + t0, 1), 0, 0)),
                  tok(N_BRANCH * d), sel(ATT_W), tok2(S5_W), tok(S5_W), tmaj(RW_W), tmaj(RW_W),
                  tok(2 * RW_W),
                  c2(1, S5_W), c2(S5_W, 2 * S5_W), c2(1, RW_W), c2(1, RW_W),
                  c2(ATT_W, d), c2(S5_W, d), c2(RW_W, d), c2(d, d),
                  c2(1, d), c2(ROUTE_ROWS, d), c2(ROUTE_ROWS, 1), c2(tt, tt)],
        out_specs=[sel(d), sel(d), route_spec, route_spec, c2(N_EXPERTS, LANES)],
        out_shape=[jax.ShapeDtypeStruct((b, nsel * tt, d), F32),
                   jax.ShapeDtypeStruct((b, nsel * tt, d), F32),
                   jax.ShapeDtypeStruct((b * nsel, 8, tt), jnp.int32),
                   jax.ShapeDtypeStruct((b * nsel, 8, tt), F32),
                   jax.ShapeDtypeStruct((N_EXPERTS, LANES), F32)],
        scratch_shapes=[pltpu.VMEM((N_EXPERTS, 1), F32)],
        compiler_params=_cparams(2),
        name="merge_route",
    )(res[0], res[1], modsel, gl, o_att, y_s5, s5u, y_rw[0], y_rw[1], rws,
      p['s5_d'].reshape(1, S5_W), p['s5_w_glu'].astype(BF16),
      p['rwkv_ln_w'].reshape(1, RW_W), p['rwkv_ln_b'].reshape(1, RW_W),
      p['proj_att'].astype(BF16), p['proj_s5'].astype(BF16), p['proj_rwkv'].astype(BF16),
      p['w_out'].astype(BF16), p['norm2'].reshape(1, d), wr, br, tri)


DMA_UNROLL = 8


def _start_rows(n_rows, make_copy, sems):
    def start(g, c):
        for u in range(DMA_UNROLL):
            r = g * DMA_UNROLL + u
            for k in range(2):
                make_copy(r, k, sems.at[k]).start(priority=k)
        return c

    lax.fori_loop(0, n_rows // DMA_UNROLL, start, 0)


def _wait_rows(n_rows, make_copy, sems):
    def wait(g, c):
        for k in range(2):
            for u in range(DMA_UNROLL):
                make_copy(g * DMA_UNROLL + u, k, sems.at[k]).wait()
        return c

    lax.fori_loop(0, n_rows // DMA_UNROLL, wait, 0)


def _row_copies(n_rows, make_copy, sems):
    _start_rows(n_rows, make_copy, sems)
    _wait_rows(n_rows, make_copy, sems)


def _dispatch_kernel(pend_ref, h_ref, dest_ref, hs_ref, zero_ref, sems, zsem):
    tt = h_ref.shape[1]
    n_blocks = hs_ref.shape[0] // MOE_BLOCK
    first = (pl.program_id(0) == 0) & (pl.program_id(1) == 0)

    def zero_copy(start):
        return pltpu.make_async_copy(zero_ref, hs_ref.at[pl.ds(pl.multiple_of(start, MOE_BLOCK), MOE_BLOCK), :],
                                     zsem)

    @pl.when(first)
    def _():
        zero_ref[...] = jnp.zeros_like(zero_ref)
        for e in range(N_EXPERTS):
            @pl.when(pend_ref[e + 1] > pend_ref[e])
            def _():
                zero_copy(pend_ref[e + 1] - MOE_BLOCK).start()
        for e in range(N_EXPERTS):
            @pl.when(pend_ref[e + 1] > pend_ref[e])
            def _():
                zero_copy(pend_ref[e + 1] - MOE_BLOCK).wait()

        def clear(blk, c):
            cp = zero_copy(blk * MOE_BLOCK)
            cp.start()
            cp.wait()
            return c

        lax.fori_loop(pend_ref[N_EXPERTS] // MOE_BLOCK, n_blocks, clear, 0)

    def row_copy(r, k, sem):
        return pltpu.make_async_copy(h_ref.at[0, pl.ds(r, 1), :],
                                     hs_ref.at[pl.ds(dest_ref[0, k, r], 1), :], sem)

    _row_copies(tt, row_copy, sems)


def moe_dispatch(h2, dest, pad_end0, n_slots, tt):
    b, t, d = h2.shape
    nsel = t // tt
    grid_spec = pltpu.PrefetchScalarGridSpec(
        num_scalar_prefetch=1,
        grid=(b, nsel),
        in_specs=[pl.BlockSpec((1, tt, d), lambda i, j, pe: (i, j, 0)),
                  pl.BlockSpec((1, 2, tt), lambda i, j, pe: (i * nsel + j, 0, 0),
                               memory_space=pltpu.SMEM)],
        out_specs=pl.BlockSpec(memory_space=pl.ANY),
        scratch_shapes=[pltpu.VMEM((MOE_BLOCK, d), F32),
                        pltpu.SemaphoreType.DMA((2,)), pltpu.SemaphoreType.DMA(())],
    )
    return pl.pallas_call(
        _dispatch_kernel,
        grid_spec=grid_spec,
        out_shape=jax.ShapeDtypeStruct((n_slots, d), F32),
        compiler_params=_cparams(2),
        name="moe_dispatch",
    )(pad_end0, h2, dest)


def _expert_kernel(be_ref, nu_ref, x_ref, wg_ref, wu_ref, wd_ref, o_ref):
    del be_ref
    used = pl.program_id(0) < nu_ref[0]

    @pl.when(used)
    def _():
        x = x_ref[...].astype(BF16)
        a = jnp.dot(x, wg_ref[0], preferred_element_type=F32)
        u = jnp.dot(x, wu_ref[0], preferred_element_type=F32)
        hmid = (a * _sigmoid(a) * u).astype(BF16)
        o_ref[...] = jnp.dot(hmid, wd_ref[0], preferred_element_type=F32)

    @pl.when(jnp.logical_not(used))
    def _():
        o_ref[...] = jnp.zeros_like(o_ref)


def moe_experts(hs, blk_expert, n_used, w_gate, w_up, w_down):
    n_slots, d = hs.shape
    f = w_gate.shape[2]
    nblk = n_slots // MOE_BLOCK
    rows = lambda i, be, nu: (jnp.minimum(i, nu[0] - 1), 0)
    wsel = lambda i, be, nu: (be[jnp.minimum(i, nu[0] - 1)], 0, 0)
    grid_spec = pltpu.PrefetchScalarGridSpec(
        num_scalar_prefetch=2,
        grid=(nblk,),
        in_specs=[pl.BlockSpec((MOE_BLOCK, d), rows),
                  pl.BlockSpec((1, d, f), wsel), pl.BlockSpec((1, d, f), wsel),
                  pl.BlockSpec((1, f, d), wsel)],
        out_specs=pl.BlockSpec((MOE_BLOCK, d), lambda i, be, nu: (i, 0)),
    )
    return pl.pallas_call(
        _expert_kernel,
        grid_spec=grid_spec,
        out_shape=jax.ShapeDtypeStruct((n_slots, d), F32),
        compiler_params=_cparams(1),
        name="moe_experts",
    )(blk_expert, n_used, hs, w_gate.astype(BF16), w_up.astype(BF16), w_down.astype(BF16))


def _combine_kernel(x_ref, mod_ref, dest_ref, dest_next_ref, gate_ref, ys_ref, o_ref, buf_ref, sems):
    tt = x_ref.shape[1]
    g = pl.program_id(0) * pl.num_programs(1) + pl.program_id(1)
    n_tiles = pl.num_programs(0) * pl.num_programs(1)
    slot = lax.rem(g, 2)

    def gather(dref, sl):
        def row_copy(r, k, sem):
            return pltpu.make_async_copy(ys_ref.at[pl.ds(dref[0, k, r], 1), :],
                                         buf_ref.at[sl, k, pl.ds(r, 1), :], sem)
        return row_copy

    @pl.when(g == 0)
    def _():
        _start_rows(tt, gather(dest_ref, 0), sems.at[0])

    @pl.when(g + 1 < n_tiles)
    def _():
        _start_rows(tt, gather(dest_next_ref, 1 - slot), sems.at[1 - slot])

    _wait_rows(tt, gather(dest_ref, slot), sems.at[slot])
    y = gate_ref[0, :, 0:1] * buf_ref[slot, 0] + gate_ref[0, :, 1:2] * buf_ref[slot, 1]
    o_ref[0] = x_ref[0] + mod_ref[0, 0, 5:6, :] * y


def moe_combine(x_mid, modsel, dest, gates_t, ys, tt, t0):
    b, t, d = x_mid.shape
    nsel = t // tt
    grid_spec = pltpu.PrefetchScalarGridSpec(
        num_scalar_prefetch=0,
        grid=(b, nsel),
        in_specs=[pl.BlockSpec((1, tt, d), lambda i, j: (i, j, 0)),
                  pl.BlockSpec((1, 1, 6, d), lambda i, j: (i, jnp.minimum(j + t0, 1), 0, 0)),
                  pl.BlockSpec((1, 2, tt), lambda i, j: (i * nsel + j, 0, 0), memory_space=pltpu.SMEM),
                  pl.BlockSpec((1, 2, tt), lambda i, j: (jnp.minimum(i * nsel + j + 1, b * nsel - 1), 0, 0),
                               memory_space=pltpu.SMEM),
                  pl.BlockSpec((1, tt, 2), lambda i, j: (i * nsel + j, 0, 0)),
                  pl.BlockSpec(memory_space=pl.ANY)],
        out_specs=pl.BlockSpec((1, tt, d), lambda i, j: (i, j, 0)),
        scratch_shapes=[pltpu.VMEM((2, 2, tt, d), F32), pltpu.SemaphoreType.DMA((2, 2))],
    )
    return pl.pallas_call(
        _combine_kernel,
        grid_spec=grid_spec,
        out_shape=jax.ShapeDtypeStruct((b, t, d), F32),
        compiler_params=_cparams(2),
        name="moe_combine",
    )(x_mid, modsel, dest, dest, gates_t, ys)


def moe_layer(x_mid, h2, ri, rg, cnt, modsel, p, tt, t0):
    b, t, d = x_mid.shape
    nsel = t // tt
    n_tok = b * nsel * tt
    n_blocks = -(-(n_tok * 2) // MOE_BLOCK) + N_EXPERTS
    n_slots = n_blocks * MOE_BLOCK
    counts = cnt[:, 0].astype(jnp.int32)
    padded = (counts + MOE_BLOCK - 1) // MOE_BLOCK * MOE_BLOCK
    pad_end = jnp.cumsum(padded)
    pad_start = pad_end - padded
    dest = pad_start[ri[:, 0:2, :]] + ri[:, 2:4, :]
    blk_start = jnp.arange(n_blocks, dtype=jnp.int32) * MOE_BLOCK
    blk_expert = jnp.minimum(jnp.sum(blk_start[:, None] >= pad_end[None, :], axis=1),
                             N_EXPERTS - 1).astype(jnp.int32)
    n_used = (pad_end[-1:] // MOE_BLOCK).astype(jnp.int32)
    pad_end0 = jnp.concatenate([jnp.zeros((1,), jnp.int32), pad_end.astype(jnp.int32)])
    hs = moe_dispatch(h2, dest, pad_end0, n_slots, tt)
    ys = moe_experts(hs, blk_expert, n_used, p['exp_w_gate'], p['exp_w_up'], p['exp_w_down'])
    gates_t = jnp.transpose(rg[:, 0:2, :], (0, 2, 1))
    return moe_combine(x_mid, modsel, dest, gates_t, ys, tt, t0)


def kernel(x, c, ctx, c_ctx, w_mod, b_mod, norm1, w_in, q_gain, k_gain, s5_a_re, s5_a_im, s5_log_dt, s5_b_re, s5_b_im, s5_c_re, s5_c_im, s5_d, s5_w_glu, rwkv_conv, rwkv_w0, rwkv_w2, rwkv_a0, rwkv_a2, rwkv_g2, rwkv_k_k, rwkv_k_a, rwkv_r_k, rwkv_ln_w, rwkv_ln_b, proj_att, proj_s5, proj_rwkv, w_out, norm2, router_g_w, router_g_b, router_e_w, router_e_b, exp_w_gate, exp_w_up, exp_w_down):
    b, l_len, d = x.shape
    c_len = ctx.shape[1]
    depth = w_mod.shape[0]
    t = c_len + l_len
    tt = 256 if c_len % 256 == 0 else c_len
    assert l_len % tt == 0 and tt % S5_CHUNK == 0 and tt % RW_TB == 0 and tt % DMA_UNROLL == 0
    assert b * RW_H == LANES, "the recurrence puts (batch, head) on the 128 lanes"

    rows = -(-(b + 1) // 8) * 8
    cc = jnp.concatenate([c, c_ctx[None], jnp.zeros((rows - b - 1, d), F32)], axis=0)
    mods = modulation(cc, w_mod, b_mod)
    cos_t, sin_t = rope_tables(c_len, l_len)

    res = (ctx, x, c_len // tt)
    for l in range(depth):
        need_ctx = l < depth - 1
        t0 = 0 if need_ctx else c_len // tt
        mod_lat = mods[l, :b].reshape(b, 1, 6, d)
        mod_ctx = jnp.broadcast_to(mods[l, b].reshape(1, 1, 6, d), (b, 1, 6, d))
        modsel = jnp.concatenate([mod_ctx, mod_lat], axis=1)

        a, s5u, z, gl = in_projection(res, t, modsel, norm1[l], w_in[l], tt)
        o_att = attention(a, cos_t, sin_t, q_gain[l], k_gain[l], tt, t0, c_len)
        ops = s5_operators(s5_a_re[l], s5_a_im[l], s5_log_dt[l], s5_b_re[l], s5_b_im[l],
                           s5_c_re[l], s5_c_im[l])
        y_s5 = s5_scan(s5u, ops, c_len)
        rws, streams = rwkv_prepare(z, rwkv_conv[l], rwkv_w0[l], rwkv_w2[l], rwkv_a0[l], rwkv_a2[l],
                                    rwkv_g2[l], rwkv_k_k[l], rwkv_k_a[l], rwkv_r_k[l], tt, c_len)
        y_rw = rwkv_scan(streams, b, c_len)
        p = dict(s5_d=s5_d[l], s5_w_glu=s5_w_glu[l], rwkv_ln_w=rwkv_ln_w[l], rwkv_ln_b=rwkv_ln_b[l],
                 proj_att=proj_att[l], proj_s5=proj_s5[l], proj_rwkv=proj_rwkv[l], w_out=w_out[l],
                 norm2=norm2[l], router_g_w=router_g_w[l], router_g_b=router_g_b[l],
                 router_e_w=router_e_w[l], router_e_b=router_e_b[l],
                 exp_w_gate=exp_w_gate[l], exp_w_up=exp_w_up[l], exp_w_down=exp_w_down[l])
        x_mid, h2, ri, rg, cnt = merge_and_route(res, t, modsel, gl, o_att, y_s5, s5u, y_rw, rws, p, tt, t0)
        xa = moe_layer(x_mid, h2, ri, rg, cnt, modsel, p, tt, t0)
        res = (xa, xa, 0)
    return xa
```
